```python
import math
import jax
import jax.numpy as jnp
from jax import lax
import numpy as np

D_MODEL = 1024
BATCH = 16
SEQ = 2048
DEPTH = 2

HEAD_DIM = 64
ROT_DIM = HEAD_DIM // 4
ROPE_THETA = 500000.0
Q_BLOCK = 128
NORM_EPS = 1e-6
SUBLN_EPS = 1e-5

SB_HEADS = 8
SB_WIDTH = SB_HEADS * HEAD_DIM

DIL_PATTERNS = ((128, 1), (512, 4), (2048, 16))
DIL_GROUPS = 3
DIL_HEADS_PER_GROUP = 4
DIL_OUT_WIDTH = DIL_HEADS_PER_GROUP * HEAD_DIM
DIL_WIDTH = DIL_GROUPS * DIL_OUT_WIDTH

DIFF_HEADS = 4
DIFF_V_DIM = 2 * HEAD_DIM
DIFF_QK_WIDTH = DIFF_HEADS * HEAD_DIM
DIFF_V_WIDTH = DIFF_HEADS * DIFF_V_DIM

N_BRANCHES = 3
GATE_WIDTH = N_BRANCHES * D_MODEL
IN_SPLIT_SIZES = (SB_WIDTH, SB_WIDTH, SB_WIDTH, DIL_WIDTH, DIL_WIDTH, DIL_WIDTH, DIFF_QK_WIDTH, DIFF_QK_WIDTH, DIFF_QK_WIDTH, DIFF_QK_WIDTH, DIFF_V_WIDTH, GATE_WIDTH)
IN_WIDTH = 3 * SB_WIDTH + 3 * DIL_WIDTH + 4 * DIFF_QK_WIDTH + DIFF_V_WIDTH + GATE_WIDTH

D_FF = 2816

kernel_name = 'hybrid_gated_sb_dilated_diff_macaron'


def rmsnorm(x, g, eps=NORM_EPS):
    xf = x.astype(jnp.float32)
    y = xf * lax.rsqrt(jnp.mean(xf * xf, axis=-1, keepdims=True) + eps)
    return (y * g.astype(jnp.float32)).astype(x.dtype)


def swiglu(x, w_in, w_out):
    gate, up = jnp.split(x @ w_in, 2, axis=-1)
    return (jax.nn.silu(gate) * up) @ w_out


def rope_tables(positions, dtype):
    inv_freq = ROPE_THETA ** (-jnp.arange(0, ROT_DIM, 2, dtype=jnp.float32) / ROT_DIM)
    ang = positions.astype(jnp.float32)[..., None] * inv_freq
    return jnp.cos(ang)[:, :, None, :].astype(dtype), jnp.sin(ang)[:, :, None, :].astype(dtype)


def apply_partial_rope(t, cos, sin):
    half = ROT_DIM // 2
    t1 = t[..., :half]
    t2 = t[..., half:ROT_DIM]
    return jnp.concatenate([t1 * cos - t2 * sin, t2 * cos + t1 * sin, t[..., ROT_DIM:]], axis=-1)


def stick_breaking_attention(q, k, v):
    B, S, H, hd = q.shape
    nb = S // Q_BLOCK
    scale = hd ** -0.5
    qb = q.reshape(B, nb, Q_BLOCK, H, hd).transpose(1, 0, 3, 2, 4)
    kh = k.transpose(0, 2, 1, 3)
    vh = v.transpose(0, 2, 1, 3)
    key_pos = jnp.arange(S)

    def block(args):
        q_blk, start = args
        q_pos = start + jnp.arange(Q_BLOCK)
        strict = key_pos[None, :] < q_pos[:, None]
        z = jnp.einsum('bhqd,bhkd->bhqk', q_blk, kh).astype(jnp.float32) * scale
        log_beta = jax.nn.log_sigmoid(z)
        log_1m_beta = jnp.where(strict, log_beta - z, 0.0)
        between = lax.cumsum(log_1m_beta, axis=3, reverse=True) - log_1m_beta
        w = jnp.where(strict, jnp.exp(log_beta + between), 0.0)
        return jnp.einsum('bhqk,bhkd->bhqd', w.astype(v.dtype), vh)

    out = lax.map(block, (qb, jnp.arange(nb) * Q_BLOCK))
    return out.transpose(1, 0, 3, 2, 4).reshape(B, S, H * hd)


def _dilated_group(q, k, v, window, dil):
    B, S, H, hd = q.shape
    span = window // dil
    L = S // dil
    Lp = -(-L // span) * span
    nb = Lp // span
    scale = hd ** -0.5

    def to_sub(t):
        t = t.reshape(B, L, dil, H, hd).transpose(0, 2, 3, 1, 4)
        t = jnp.pad(t, ((0, 0), (0, 0), (0, 0), (0, Lp - L), (0, 0)))
        return t.reshape(B, dil, H, nb, span, hd)

    def with_prev(t):
        prev = jnp.pad(t, ((0, 0), (0, 0), (0, 0), (1, 0), (0, 0), (0, 0)))[:, :, :, :-1]
        return jnp.concatenate([prev, t], axis=4)

    qs = to_sub(q)
    kband = with_prev(to_sub(k))
    vband = with_prev(to_sub(v))
    s = jnp.einsum('bchnqd,bchnkd->bchnqk', qs, kband).astype(jnp.float32) * scale
    qi = jnp.arange(span)[:, None] + span
    kj = jnp.arange(2 * span)[None, :]
    dist = qi - kj
    blk = jnp.arange(nb)[:, None, None]
    valid = (dist >= 0) & (dist <= span) & ((blk > 0) | (kj >= span))
    s = jnp.where(valid, s, -jnp.inf)
    lse = jax.nn.logsumexp(s, axis=-1)
    p = jnp.exp(s - lse[..., None])
    o = jnp.einsum('bchnqk,bchnkd->bchnqd', p.astype(v.dtype), vband)
    o = o.reshape(B, dil, H, Lp, hd)[:, :, :, :L].transpose(0, 3, 1, 2, 4).reshape(B, S, H, hd)
    lse = lse.reshape(B, dil, H, Lp)[:, :, :, :L].transpose(0, 3, 1, 2).reshape(B, S, H)
    return o, lse


def dilated_window_attention(q, k, v):
    B, S, _, hd = q.shape
    outs = []
    lses = []
    for g, (window, dil) in enumerate(DIL_PATTERNS):
        sl = slice(g * DIL_HEADS_PER_GROUP, (g + 1) * DIL_HEADS_PER_GROUP)
        o, lse = _dilated_group(q[:, :, sl], k[:, :, sl], v[:, :, sl], window, dil)
        outs.append(o)
        lses.append(lse)
    wts = jax.nn.softmax(jnp.stack(lses, axis=0), axis=0)
    out = jnp.sum(wts[..., None].astype(q.dtype) * jnp.stack(outs, axis=0), axis=0)
    return out.reshape(B, S, DIL_OUT_WIDTH)


def diff_attention(q1, q2, k1, k2, v, lam):
    B, S, H, hd = q1.shape
    nb = S // Q_BLOCK
    scale = hd ** -0.5

    def to_blocks(t):
        return t.reshape(B, nb, Q_BLOCK, H, hd).transpose(1, 0, 3, 2, 4)

    k1h = k1.transpose(0, 2, 1, 3)
    k2h = k2.transpose(0, 2, 1, 3)
    vh = v.transpose(0, 2, 1, 3)
    key_pos = jnp.arange(S)

    def block(args):
        q1b, q2b, start = args
        q_pos = start + jnp.arange(Q_BLOCK)
        causal = key_pos[None, :] <= q_pos[:, None]
        s1 = jnp.einsum('bhqd,bhkd->bhqk', q1b, k1h).astype(jnp.float32) * scale
        s2 = jnp.einsum('bhqd,bhkd->bhqk', q2b, k2h).astype(jnp.float32) * scale
        p1 = jax.nn.softmax(jnp.where(causal, s1, -jnp.inf), axis=-1)
        p2 = jax.nn.softmax(jnp.where(causal, s2, -jnp.inf), axis=-1)
        w = (p1 - lam * p2).astype(v.dtype)
        return jnp.einsum('bhqk,bhkd->bhqd', w, vh)

    out = lax.map(block, (to_blocks(q1), to_blocks(q2), jnp.arange(nb) * Q_BLOCK))
    return out.transpose(1, 0, 3, 2, 4).reshape(B, S, H, 2 * hd)


def setup_inputs(seed: int = 0) -> dict:
    key = jax.random.key(seed)
    ks = jax.random.split(key, 24)
    f32 = jnp.float32

    def dense(k, fan_in, fan_out):
        return jax.random.normal(k, (DEPTH, fan_in, fan_out), f32) * fan_in ** -0.5

    def gain(k, shape):
        return 1.0 + 0.01 * jax.random.normal(k, shape, f32)

    x = jax.random.normal(ks[0], (BATCH, SEQ, D_MODEL), f32)
    start = jax.random.randint(ks[1], (BATCH, 1), 0, 4096, dtype=jnp.int32)
    positions = start + jnp.arange(SEQ, dtype=jnp.int32)[None, :]
    return {
        'x': x,
        'positions': positions,
        'ffn1_norm': gain(ks[2], (DEPTH, D_MODEL)),
        'ffn1_w_in': dense(ks[3], D_MODEL, 2 * D_FF),
        'ffn1_w_out': dense(ks[4], D_FF, D_MODEL),
        'mix_norm': gain(ks[5], (DEPTH, D_MODEL)),
        'w_in': dense(ks[6], D_MODEL, IN_WIDTH),
        'b_gate': 0.01 * jax.random.normal(ks[7], (DEPTH, GATE_WIDTH), f32),
        'lam_q1': 0.1 * jax.random.normal(ks[8], (DEPTH, HEAD_DIM), f32),
        'lam_k1': 0.1 * jax.random.normal(ks[9], (DEPTH, HEAD_DIM), f32),
        'lam_q2': 0.1 * jax.random.normal(ks[10], (DEPTH, HEAD_DIM), f32),
        'lam_k2': 0.1 * jax.random.normal(ks[11], (DEPTH, HEAD_DIM), f32),
        'diff_subln': gain(ks[12], (DEPTH, DIFF_V_DIM)),
        'w_up_a': dense(ks[13], SB_WIDTH, D_MODEL),
        'w_up_b': dense(ks[14], DIL_OUT_WIDTH, D_MODEL),
        'w_up_c': dense(ks[15], DIFF_V_WIDTH, D_MODEL),
        'w_out': dense(ks[16], D_MODEL, D_MODEL),
        'ffn2_norm': gain(ks[17], (DEPTH, D_MODEL)),
        'ffn2_w_in': dense(ks[18], D_MODEL, 2 * D_FF),
        'ffn2_w_out': dense(ks[19], D_FF, D_MODEL),
        'final_norm': gain(ks[20], (D_MODEL,)),
    }


def reference(x, positions, ffn1_norm, ffn1_w_in, ffn1_w_out, mix_norm, w_in, b_gate, lam_q1, lam_k1, lam_q2, lam_k2, diff_subln, w_up_a, w_up_b, w_up_c, w_out, ffn2_norm, ffn2_w_in, ffn2_w_out, final_norm):
    B, S, D = x.shape
    cos, sin = rope_tables(positions, x.dtype)
    split_at = np.cumsum(IN_SPLIT_SIZES)[:-1].tolist()
    for l in range(DEPTH):
        x = x + 0.5 * swiglu(rmsnorm(x, ffn1_norm[l]), ffn1_w_in[l], ffn1_w_out[l])

        h = rmsnorm(x, mix_norm[l])
        (qa, ka, va, qb, kb, vb, q1, q2, k1, k2, vc, gate_pre) = jnp.split(h @ w_in[l], split_at, axis=-1)

        y_a = stick_breaking_attention(qa.reshape(B, S, SB_HEADS, HEAD_DIM), ka.reshape(B, S, SB_HEADS, HEAD_DIM), va.reshape(B, S, SB_HEADS, HEAD_DIM))

        hb = DIL_GROUPS * DIL_HEADS_PER_GROUP
        y_b = dilated_window_attention(apply_partial_rope(qb.reshape(B, S, hb, HEAD_DIM), cos, sin), apply_partial_rope(kb.reshape(B, S, hb, HEAD_DIM), cos, sin), vb.reshape(B, S, hb, HEAD_DIM))

        lam_init = 0.8 - 0.6 * math.exp(-0.3 * l)
        lam = (jnp.exp(jnp.sum(lam_q1[l].astype(jnp.float32) * lam_k1[l].astype(jnp.float32))) - jnp.exp(jnp.sum(lam_q2[l].astype(jnp.float32) * lam_k2[l].astype(jnp.float32))) + lam_init)
        rs = lambda t: apply_partial_rope(t.reshape(B, S, DIFF_HEADS, HEAD_DIM), cos, sin)
        o_c = diff_attention(rs(q1), rs(q2), rs(k1), rs(k2), vc.reshape(B, S, DIFF_HEADS, DIFF_V_DIM), lam)
        y_c = (rmsnorm(o_c, diff_subln[l], SUBLN_EPS) * (1.0 - lam_init)).reshape(B, S, DIFF_V_WIDTH)

        gates = jax.nn.sigmoid(gate_pre + b_gate[l]).reshape(B, S, N_BRANCHES, D)
        merged = (gates[:, :, 0] * (y_a @ w_up_a[l]) + gates[:, :, 1] * (y_b @ w_up_b[l]) + gates[:, :, 2] * (y_c @ w_up_c[l]))
        x = x + merged @ w_out[l]

        x = x + 0.5 * swiglu(rmsnorm(x, ffn2_norm[l]), ffn2_w_in[l], ffn2_w_out[l])
    return rmsnorm(x, final_norm)
```

```python
import functools
import math

import jax
import jax.numpy as jnp
from jax import lax
from jax.experimental import pallas as pl
from jax.experimental.pallas import tpu as pltpu

F32 = jnp.float32
BF16 = jnp.bfloat16

D_MODEL = 1024
DEPTH = 2
HEAD_DIM = 64
ROT_DIM = HEAD_DIM // 4
ROT_HALF = ROT_DIM // 2
ROPE_THETA = 500000.0
NORM_EPS = 1e-6
SUBLN_EPS = 1e-5
QK_SCALE = HEAD_DIM ** -0.5

SB_HEADS = 8
SB_WIDTH = SB_HEADS * HEAD_DIM
DIL_PATTERNS = ((128, 1), (512, 4), (2048, 16))
DIL_SPAN = 128
DIL_GROUP_WIDTH = 4 * HEAD_DIM
DIL_WIDTH = 3 * DIL_GROUP_WIDTH
DIFF_HEADS = 4
DIFF_QK_WIDTH = DIFF_HEADS * HEAD_DIM
DIFF_V_DIM = 2 * HEAD_DIM
DIFF_V_WIDTH = DIFF_HEADS * DIFF_V_DIM
GATE_WIDTH = 3 * D_MODEL
D_FF = 2816

LANES = 128
VMEM_LIMIT = 56 * 1024 * 1024

ROPE_COLS = 2 * DIL_WIDTH + 4 * DIFF_QK_WIDTH
PLAIN_COLS = 3 * SB_WIDTH + DIL_WIDTH + DIFF_V_WIDTH
R_QB, R_KB = 0, DIL_WIDTH
R_Q1 = 2 * DIL_WIDTH
R_Q2 = R_Q1 + DIFF_QK_WIDTH
R_K1 = R_Q2 + DIFF_QK_WIDTH
R_K2 = R_K1 + DIFF_QK_WIDTH
P_QA, P_KA, P_VA = 0, SB_WIDTH, 2 * SB_WIDTH
P_VB = 3 * SB_WIDTH
P_VC = P_VB + DIL_WIDTH


def _params(*sem):
    return pltpu.CompilerParams(dimension_semantics=sem, vmem_limit_bytes=VMEM_LIMIT)


def _rmsnorm(x, g, eps):
    ms = jnp.mean(x * x, axis=-1, keepdims=True)
    return x * lax.rsqrt(ms + eps) * g


def _rope_table_kernel(pos_ref, invf_ref, cos_ref, sin_ref):
    ang = pos_ref[...].astype(F32) * invf_ref[...]
    lane = lax.broadcasted_iota(jnp.int32, ang.shape, 1) % HEAD_DIM
    c = jnp.cos(ang)
    s = jnp.sin(ang)
    cos_ref[...] = jnp.where(lane < ROT_DIM, c, 1.0)
    sin_ref[...] = jnp.where(lane < ROT_HALF, -s, jnp.where(lane < ROT_DIM, s, 0.0))


def _rope_tables(positions):
    rows = positions.size
    tm = 1024
    inv_freq = ROPE_THETA ** (-jnp.arange(0, ROT_DIM, 2, dtype=F32) / ROT_DIM)
    lane = jnp.arange(LANES) % HEAD_DIM
    invf = jnp.where(lane < ROT_DIM, inv_freq[lane % ROT_HALF], 0.0).reshape(1, LANES)
    return pl.pallas_call(
        _rope_table_kernel,
        grid=(rows // tm,),
        in_specs=[pl.BlockSpec((tm, 1), lambda i: (i, 0)),
                  pl.BlockSpec((1, LANES), lambda i: (0, 0))],
        out_specs=[pl.BlockSpec((tm, LANES), lambda i: (i, 0))] * 2,
        out_shape=[jax.ShapeDtypeStruct((rows, LANES), F32)] * 2,
        compiler_params=_params("parallel"),
        name="rope_tables",
    )(positions.reshape(rows, 1), invf)


def _ffn_kernel(x_ref, g_ref, wg_ref, wu_ref, wo_ref, fg_ref, o_ref, n_ref, acc_ref, *, nj, final):
    j = pl.program_id(1)

    @pl.when(j == 0)
    def _():
        n_ref[...] = _rmsnorm(x_ref[...], g_ref[...], NORM_EPS).astype(BF16)

    n = n_ref[...]
    gate = jnp.dot(n, wg_ref[...], preferred_element_type=F32)
    up = jnp.dot(n, wu_ref[...], preferred_element_type=F32)
    h = (gate * jax.nn.sigmoid(gate) * up).astype(BF16)
    part = jnp.dot(h, wo_ref[...], preferred_element_type=F32)

    @pl.when(j == 0)
    def _():
        acc_ref[...] = part

    @pl.when(j > 0)
    def _():
        acc_ref[...] += part

    @pl.when(j == nj - 1)
    def _():
        y = x_ref[...] + 0.5 * acc_ref[...]
        if final:
            y = _rmsnorm(y, fg_ref[...], NORM_EPS)
        o_ref[...] = y


def _ffn(x, gain, w_in, w_out, final_gain, final):
    rows = x.shape[0]
    tm, tf = 512, 1408
    nj = D_FF // tf
    return pl.pallas_call(
        functools.partial(_ffn_kernel, nj=nj, final=final),
        grid=(rows // tm, nj),
        in_specs=[pl.BlockSpec((tm, D_MODEL), lambda i, j: (i, 0)),
                  pl.BlockSpec((1, D_MODEL), lambda i, j: (0, 0)),
                  pl.BlockSpec((D_MODEL, tf), lambda i, j: (0, j)),
                  pl.BlockSpec((D_MODEL, tf), lambda i, j, nj=nj: (0, j + nj)),
                  pl.BlockSpec((tf, D_MODEL), lambda i, j: (j, 0)),
                  pl.BlockSpec((1, D_MODEL), lambda i, j: (0, 0))],
        out_specs=pl.BlockSpec((tm, D_MODEL), lambda i, j: (i, 0)),
        out_shape=jax.ShapeDtypeStruct((rows, D_MODEL), F32),
        scratch_shapes=[pltpu.VMEM((tm, D_MODEL), BF16), pltpu.VMEM((tm, D_MODEL), F32)],
        compiler_params=_params("parallel", "arbitrary"),
        name="ffn_final" if final else "ffn",
    )(x, gain, w_in, w_in, w_out, final_gain)


def _proj_kernel(x_ref, g_ref, w_ref, sc_ref, *rest, rope):
    if rope:
        cos_ref, sin_ref, o_ref, n_ref = rest
    else:
        o_ref, n_ref = rest
    j = pl.program_id(1)

    @pl.when(j == 0)
    def _():
        n_ref[...] = _rmsnorm(x_ref[...], g_ref[...], NORM_EPS).astype(BF16)

    p = jnp.dot(n_ref[...], w_ref[...], preferred_element_type=F32) * sc_ref[...]
    if not rope:
        o_ref[...] = p.astype(o_ref.dtype)
        return
    cos = cos_ref[...]
    sin = sin_ref[...]
    first_half = lax.broadcasted_iota(jnp.int32, cos.shape, 1) % HEAD_DIM < ROT_HALF
    for c in range(p.shape[1] // LANES):
        t = p[:, c * LANES:(c + 1) * LANES]
        partner = jnp.where(first_half,
                            pltpu.roll(t, LANES - ROT_HALF, axis=1),
                            pltpu.roll(t, ROT_HALF, axis=1))
        o_ref[:, c * LANES:(c + 1) * LANES] = (t * cos + partner * sin).astype(o_ref.dtype)


def _proj(x, gain, w, scale, tables, tn):
    rows = x.shape[0]
    tm = 512
    cols = w.shape[1]
    rope = tables is not None
    in_specs = [pl.BlockSpec((tm, D_MODEL), lambda i, j: (i, 0)),
                pl.BlockSpec((1, D_MODEL), lambda i, j: (0, 0)),
                pl.BlockSpec((D_MODEL, tn), lambda i, j: (0, j)),
                pl.BlockSpec((1, tn), lambda i, j: (0, j))]
    args = [x, gain, w, scale]
    if rope:
        in_specs += [pl.BlockSpec((tm, LANES), lambda i, j: (i, 0))] * 2
        args += list(tables)
    return pl.pallas_call(
        functools.partial(_proj_kernel, rope=rope),
        grid=(rows // tm, cols // tn),
        in_specs=in_specs,
        out_specs=pl.BlockSpec((tm, tn), lambda i, j: (i, j)),
        out_shape=jax.ShapeDtypeStruct((rows, cols), BF16),
        scratch_shapes=[pltpu.VMEM((tm, D_MODEL), BF16)],
        compiler_params=_params("parallel", "arbitrary"),
        name="proj_rope" if rope else "proj_plain",
    )(*args)


def _nt_dot(a, b):
    return lax.dot_general(a, b, (((1,), (1,)), ((), ())), preferred_element_type=F32)


def _sb_kernel(q_ref, k_ref, v_ref, o_ref, *, tq):
    i = pl.program_id(2)
    q2 = q_ref[...]
    lane = lax.broadcasted_iota(jnp.int32, (1, LANES), 1)
    qms = [jnp.where(lane < HEAD_DIM, q2, jnp.zeros_like(q2)),
           jnp.where(lane >= HEAD_DIM, q2, jnp.zeros_like(q2))]
    row = lax.broadcasted_iota(jnp.int32, (tq, tq), 0)
    col = lax.broadcasted_iota(jnp.int32, (tq, tq), 1)
    later_key = (row > col).astype(BF16)
    strict = col < row

    def block(j, state, masked):
        start = pl.multiple_of(j * tq, tq)
        kb = k_ref[pl.ds(start, tq), :]
        vb = v_ref[pl.ds(start, tq), :]
        out = []
        for h in range(2):
            acc, carry = state[h]
            z = _nt_dot(qms[h], kb)
            lp = jnp.log1p(jnp.exp(-jnp.abs(z)))
            log_1m_beta = -(jnp.maximum(z, 0.0) + lp)
            log_beta = jnp.minimum(z, 0.0) - lp
            if masked:
                log_1m_beta = jnp.where(strict, log_1m_beta, 0.0)
            hi = log_1m_beta.astype(BF16)
            lo = (log_1m_beta - hi.astype(F32)).astype(BF16)
            between = (jnp.dot(hi, later_key, preferred_element_type=F32)
                       + jnp.dot(lo, later_key, preferred_element_type=F32) + carry)
            w = jnp.exp(log_beta + between)
            if masked:
                w = jnp.where(strict, w, 0.0)
            acc = acc + jnp.dot(w.astype(BF16), vb, preferred_element_type=F32)
            carry = carry + jnp.sum(log_1m_beta, axis=-1, keepdims=True)
            out.append((acc, carry))
        return tuple(out)

    zero = (jnp.zeros((tq, LANES), F32), jnp.zeros((tq, 1), F32))
    state = block(i, (zero, zero), True)
    state = lax.fori_loop(0, i, lambda jj, st: block(i - 1 - jj, st, False), state)
    o_ref[...] = jnp.where(lane < HEAD_DIM, state[0][0], state[1][0]).astype(o_ref.dtype)


def _stick_breaking(p_plain, batch, seq):
    tq = 128
    nq = seq // tq
    pairs = SB_WIDTH // LANES
    kcol, vcol = P_KA // LANES, P_VA // LANES
    return pl.pallas_call(
        functools.partial(_sb_kernel, tq=tq),
        grid=(batch, pairs, nq),
        in_specs=[pl.BlockSpec((tq, LANES), lambda b, p, i: (b * nq + i, p)),
                  pl.BlockSpec((seq, LANES), lambda b, p, i: (b, kcol + p)),
                  pl.BlockSpec((seq, LANES), lambda b, p, i: (b, vcol + p))],
        out_specs=pl.BlockSpec((tq, LANES), lambda b, p, i: (b * nq + i, p)),
        out_shape=jax.ShapeDtypeStruct((batch * seq, SB_WIDTH), BF16),
        compiler_params=_params("parallel", "parallel", "arbitrary"),
        name="stick_breaking",
    )(p_plain, p_plain, p_plain)


def _dil_kernel(q_ref, kp_ref, kc_ref, vp_ref, vc_ref, o_ref, lse_ref):
    n = pl.program_id(2)
    q = q_ref[...]
    kcat = jnp.concatenate([kp_ref[...], kc_ref[...]], axis=0)
    vcat = jnp.concatenate([vp_ref[...], vc_ref[...]], axis=0)
    span = DIL_SPAN
    qi = lax.broadcasted_iota(jnp.int32, (span, 2 * span), 0) + span
    kj = lax.broadcasted_iota(jnp.int32, (span, 2 * span), 1)
    dist = qi - kj
    first_key = jnp.where(n > 0, 0, span)
    valid = (dist >= 0) & (dist <= span) & (kj >= first_key)
    lane = lax.broadcasted_iota(jnp.int32, (1, DIL_GROUP_WIDTH), 1)
    out = jnp.zeros((span, DIL_GROUP_WIDTH), F32)
    lse_out = jnp.zeros((span, DIL_GROUP_WIDTH), F32)
    for h in range(DIL_GROUP_WIDTH // HEAD_DIM):
        mine = (lane >= h * HEAD_DIM) & (lane < (h + 1) * HEAD_DIM)
        s = _nt_dot(jnp.where(mine, q, jnp.zeros_like(q)), kcat)
        s = jnp.where(valid, s, -jnp.inf)
        m = jnp.max(s, axis=-1, keepdims=True)
        p = jnp.exp(s - m)
        l = jnp.sum(p, axis=-1, keepdims=True)
        o = jnp.dot((p / l).astype(BF16), vcat, preferred_element_type=F32)
        out = jnp.where(mine, o, out)
        lse_out = jnp.where(mine, m + jnp.log(l), lse_out)
    o_ref[...] = out
    lse_ref[...] = lse_out


def _dilated_group(q, k, v, batch, seq, dil):
    span = DIL_SPAN
    sub = seq // dil
    nb = sub // span
    gw = DIL_GROUP_WIDTH
    regroup = lambda t: t.reshape(batch * sub, dil * gw)
    cur = pl.BlockSpec((span, gw), lambda b, c, n: (b * nb + n, c))
    prev = pl.BlockSpec((span, gw), lambda b, c, n: (b * nb + jnp.maximum(n - 1, 0), c))
    o, lse = pl.pallas_call(
        _dil_kernel,
        grid=(batch, dil, nb),
        in_specs=[cur, prev, cur, prev, cur],
        out_specs=[cur, cur],
        out_shape=[jax.ShapeDtypeStruct((batch * sub, dil * gw), F32)] * 2,
        compiler_params=_params("parallel", "parallel", "arbitrary"),
        name=f"dilated_d{dil}",
    )(regroup(q), regroup(k), regroup(k), regroup(v), regroup(v))
    return o.reshape(batch * seq, gw), lse.reshape(batch * seq, gw)


def _diff_kernel(q1_ref, q2_ref, k1_ref, k2_ref, v_ref, lq1_ref, lk1_ref, lq2_ref, lk2_ref, g_ref,
                 o_ref, *, tq, lam_init):
    h = pl.program_id(1)
    i = pl.program_id(2)
    lane = lax.broadcasted_iota(jnp.int32, (1, LANES), 1)
    mine = lane // HEAD_DIM == h % 2
    qs = [jnp.where(mine, r[...], jnp.zeros_like(r[...])) for r in (q1_ref, q2_ref)]
    k_refs = (k1_ref, k2_ref)
    row = lax.broadcasted_iota(jnp.int32, (tq, tq), 0)
    col = lax.broadcasted_iota(jnp.int32, (tq, tq), 1)
    causal = col <= row

    def block(j, state, masked):
        start = pl.multiple_of(j * tq, tq)
        vb = v_ref[pl.ds(start, tq), :]
        out = []
        for t in range(2):
            m_old, l_old, acc = state[t]
            s = _nt_dot(qs[t], k_refs[t][pl.ds(start, tq), :])
            if masked:
                s = jnp.where(causal, s, -jnp.inf)
            m_new = jnp.maximum(m_old, jnp.max(s, axis=-1, keepdims=True))
            alpha = jnp.exp(m_old - m_new)
            p = jnp.exp(s - m_new)
            l_new = alpha * l_old + jnp.sum(p, axis=-1, keepdims=True)
            acc = alpha * acc + jnp.dot(p.astype(BF16), vb, preferred_element_type=F32)
            out.append((m_new, l_new, acc))
        return tuple(out)

    init = (jnp.full((tq, 1), -jnp.inf, F32), jnp.zeros((tq, 1), F32), jnp.zeros((tq, LANES), F32))
    state = block(i, (init, init), True)
    state = lax.fori_loop(0, i, lambda jj, st: block(i - 1 - jj, st, False), state)

    lam = (jnp.exp(jnp.sum(lq1_ref[...] * lk1_ref[...], axis=-1, keepdims=True))
           - jnp.exp(jnp.sum(lq2_ref[...] * lk2_ref[...], axis=-1, keepdims=True)) + lam_init)
    o = state[0][2] / state[0][1] - lam * (state[1][2] / state[1][1])
    o_ref[...] = (_rmsnorm(o, g_ref[...], SUBLN_EPS) * (1.0 - lam_init)).astype(o_ref.dtype)


def _diff_attention(p_rope, p_plain, lam_vecs, subln, batch, seq, lam_init):
    tq = 128
    nq = seq // tq
    q1c, q2c, k1c, k2c = (c // LANES for c in (R_Q1, R_Q2, R_K1, R_K2))
    vcol = P_VC // LANES
    qspec = lambda c0: pl.BlockSpec((tq, LANES), lambda b, h, i: (b * nq + i, c0 + h // 2))
    kspec = lambda c0: pl.BlockSpec((seq, LANES), lambda b, h, i: (b, c0 + h // 2))
    vec = lambda w: pl.BlockSpec((1, w), lambda b, h, i: (0, 0))
    return pl.pallas_call(
        functools.partial(_diff_kernel, tq=tq, lam_init=lam_init),
        grid=(batch, DIFF_HEADS, nq),
        in_specs=[qspec(q1c), qspec(q2c), kspec(k1c), kspec(k2c),
                  pl.BlockSpec((seq, LANES), lambda b, h, i: (b, vcol + h)),
                  vec(HEAD_DIM), vec(HEAD_DIM), vec(HEAD_DIM), vec(HEAD_DIM), vec(DIFF_V_DIM)],
        out_specs=pl.BlockSpec((tq, LANES), lambda b, h, i: (b * nq + i, h)),
        out_shape=jax.ShapeDtypeStruct((batch * seq, DIFF_V_WIDTH), BF16),
        compiler_params=_params("parallel", "parallel", "arbitrary"),
        name="diff_attention",
    )(p_rope, p_rope, p_rope, p_rope, p_plain, *lam_vecs, subln)


def _mix_kernel(x_ref, g_ref, wg_ref, bg_ref, ya_ref, yc_ref,
                o1_ref, o2_ref, o3_ref, s1_ref, s2_ref, s3_ref,
                wa_ref, wb_ref, wc_ref, wo_ref, out_ref):
    x = x_ref[...]
    hn = _rmsnorm(x, g_ref[...], NORM_EPS).astype(BF16)
    gates = jax.nn.sigmoid(jnp.dot(hn, wg_ref[...], preferred_element_type=F32) + bg_ref[...])
    lses = [s1_ref[...], s2_ref[...], s3_ref[...]]
    m = jnp.maximum(jnp.maximum(lses[0], lses[1]), lses[2])
    es = [jnp.exp(s - m) for s in lses]
    den = es[0] + es[1] + es[2]
    yb = (es[0] / den) * o1_ref[...] + (es[1] / den) * o2_ref[...] + (es[2] / den) * o3_ref[...]
    up_a = jnp.dot(ya_ref[...], wa_ref[...], preferred_element_type=F32)
    up_b = jnp.dot(yb.astype(BF16), wb_ref[...], preferred_element_type=F32)
    up_c = jnp.dot(yc_ref[...], wc_ref[...], preferred_element_type=F32)
    d = D_MODEL
    merged = gates[:, :d] * up_a + gates[:, d:2 * d] * up_b + gates[:, 2 * d:] * up_c
    out_ref[...] = x + jnp.dot(merged.astype(BF16), wo_ref[...], preferred_element_type=F32)


def _mix_out(x, gain, w_gate, b_gate, y_a, y_c, dil_o, dil_lse, w_up_a, w_up_b, w_up_c, w_out):
    rows = x.shape[0]
    tm = 256
    row = lambda w: pl.BlockSpec((tm, w), lambda i: (i, 0))
    full = lambda a: pl.BlockSpec(a.shape, lambda i: (0, 0))
    gw = DIL_GROUP_WIDTH
    return pl.pallas_call(
        _mix_kernel,
        grid=(rows // tm,),
        in_specs=[row(D_MODEL), full(gain), full(w_gate), full(b_gate), row(SB_WIDTH), row(DIFF_V_WIDTH),
                  row(gw), row(gw), row(gw), row(gw), row(gw), row(gw),
                  full(w_up_a), full(w_up_b), full(w_up_c), full(w_out)],
        out_specs=row(D_MODEL),
        out_shape=jax.ShapeDtypeStruct((rows, D_MODEL), F32),
        compiler_params=_params("parallel"),
        name="mix_out",
    )(x, gain, w_gate, b_gate, y_a, y_c, *dil_o, *dil_lse, w_up_a, w_up_b, w_up_c, w_out)


def _column_scale(width, scaled_ranges):
    col = jnp.arange(width)
    s = jnp.ones((width,), F32)
    for lo, hi in scaled_ranges:
        s = jnp.where((col >= lo) & (col < hi), QK_SCALE, s)
    return s.reshape(1, width)


def kernel(x, positions, ffn1_norm, ffn1_w_in, ffn1_w_out, mix_norm, w_in, b_gate, lam_q1, lam_k1, lam_q2, lam_k2, diff_subln, w_up_a, w_up_b, w_up_c, w_out, ffn2_norm, ffn2_w_in, ffn2_w_out, final_norm):
    batch, seq, d = x.shape
    rows = batch * seq
    x = x.reshape(rows, d)
    tables = _rope_tables(positions)
    vec = lambda t: t.reshape(1, -1)

    o_qa, o_ka, o_va = 0, SB_WIDTH, 2 * SB_WIDTH
    o_qb = 3 * SB_WIDTH
    o_kb, o_vb = o_qb + DIL_WIDTH, o_qb + 2 * DIL_WIDTH
    o_q1 = o_qb + 3 * DIL_WIDTH
    o_vc = o_q1 + 4 * DIFF_QK_WIDTH
    o_gate = o_vc + DIFF_V_WIDTH
    scale_rope = _column_scale(ROPE_COLS, [(R_QB, R_QB + DIL_WIDTH), (R_Q1, R_K1)])
    scale_plain = _column_scale(PLAIN_COLS, [(P_QA, P_QA + SB_WIDTH)])

    for l in range(DEPTH):
        wl = w_in[l]
        w_rope = jnp.concatenate([wl[:, o_qb:o_vb], wl[:, o_q1:o_vc]], axis=1).astype(BF16)
        w_plain = jnp.concatenate([wl[:, o_qa:o_qb], wl[:, o_vb:o_q1], wl[:, o_vc:o_gate]], axis=1).astype(BF16)
        w_gate = wl[:, o_gate:].astype(BF16)

        x = _ffn(x, vec(ffn1_norm[l]), ffn1_w_in[l].astype(BF16), ffn1_w_out[l].astype(BF16),
                 vec(final_norm), False)

        gain = vec(mix_norm[l])
        p_rope = _proj(x, gain, w_rope, scale_rope, tables, ROPE_COLS // 2)
        p_plain = _proj(x, gain, w_plain, scale_plain, None, PLAIN_COLS // 2)

        y_a = _stick_breaking(p_plain, batch, seq)

        dil_o, dil_lse = [], []
        for g, (window, dil) in enumerate(DIL_PATTERNS):
            assert window // dil == DIL_SPAN
            c = g * DIL_GROUP_WIDTH
            o, lse = _dilated_group(p_rope[:, R_QB + c:R_QB + c + DIL_GROUP_WIDTH],
                                    p_rope[:, R_KB + c:R_KB + c + DIL_GROUP_WIDTH],
                                    p_plain[:, P_VB + c:P_VB + c + DIL_GROUP_WIDTH],
                                    batch, seq, dil)
            dil_o.append(o)
            dil_lse.append(lse)

        lam_init = 0.8 - 0.6 * math.exp(-0.3 * l)
        y_c = _diff_attention(p_rope, p_plain,
                              [vec(t[l]) for t in (lam_q1, lam_k1, lam_q2, lam_k2)],
                              vec(diff_subln[l]), batch, seq, lam_init)

        x = _mix_out(x, gain, w_gate, vec(b_gate[l]), y_a, y_c, dil_o, dil_lse,
                     w_up_a[l].astype(BF16), w_up_b[l].astype(BF16), w_up_c[l].astype(BF16),
                     w_out[l].astype(BF16))

        x = _ffn(x, vec(ffn2_norm[l]), ffn2_w_in[l].astype(BF16), ffn2_w_out[l].astype(BF16),
                 vec(final_norm), l == DEPTH - 1)
    return x.reshape(batch, seq, d)
```

```python
import functools
import math

import jax
import jax.numpy as jnp
from jax import lax
from jax.experimental import pallas as pl
from jax.experimental.pallas import tpu as pltpu

F32 = jnp.float32
BF16 = jnp.bfloat16

D_MODEL = 1024
DEPTH = 2
HEAD_DIM = 64
ROT_DIM = HEAD_DIM // 4
ROT_HALF = ROT_DIM // 2
ROPE_THETA = 500000.0
NORM_EPS = 1e-6
SUBLN_EPS = 1e-5
QK_SCALE = HEAD_DIM ** -0.5

SB_HEADS = 8
SB_WIDTH = SB_HEADS * HEAD_DIM
DIL_PATTERNS = ((128, 1), (512, 4), (2048, 16))
DIL_SPAN = 128
DIL_GROUP_WIDTH = 4 * HEAD_DIM
DIL_WIDTH = 3 * DIL_GROUP_WIDTH
DIFF_HEADS = 4
DIFF_QK_WIDTH = DIFF_HEADS * HEAD_DIM
DIFF_V_DIM = 2 * HEAD_DIM
DIFF_V_WIDTH = DIFF_HEADS * DIFF_V_DIM
GATE_WIDTH = 3 * D_MODEL
D_FF = 2816

SB_NEGLIGIBLE = -104.0
LANES = 128
VMEM_LIMIT = 56 * 1024 * 1024

ROPE_COLS = 2 * DIL_WIDTH + 4 * DIFF_QK_WIDTH
PLAIN_COLS = 3 * SB_WIDTH + DIL_WIDTH + DIFF_V_WIDTH
R_QB, R_KB = 0, DIL_WIDTH
R_Q1 = 2 * DIL_WIDTH
R_Q2 = R_Q1 + DIFF_QK_WIDTH
R_K1 = R_Q2 + DIFF_QK_WIDTH
R_K2 = R_K1 + DIFF_QK_WIDTH
P_QA, P_KA, P_VA = 0, SB_WIDTH, 2 * SB_WIDTH
P_VB = 3 * SB_WIDTH
P_VC = P_VB + DIL_WIDTH


def _params(*sem):
    return pltpu.CompilerParams(dimension_semantics=sem, vmem_limit_bytes=VMEM_LIMIT)


def _rmsnorm(x, g, eps):
    ms = jnp.mean(x * x, axis=-1, keepdims=True)
    return x * lax.rsqrt(ms + eps) * g


def _rope_table_kernel(pos_ref, invf_ref, cos_ref, sin_ref):
    ang = pos_ref[...].astype(F32) * invf_ref[...]
    lane = lax.broadcasted_iota(jnp.int32, ang.shape, 1) % HEAD_DIM
    c = jnp.cos(ang)
    s = jnp.sin(ang)
    cos_ref[...] = jnp.where(lane < ROT_DIM, c, 1.0)
    sin_ref[...] = jnp.where(lane < ROT_HALF, -s, jnp.where(lane < ROT_DIM, s, 0.0))


def _rope_tables(positions):
    rows = positions.size
    tm = 1024
    inv_freq = ROPE_THETA ** (-jnp.arange(0, ROT_DIM, 2, dtype=F32) / ROT_DIM)
    lane = jnp.arange(LANES) % HEAD_DIM
    invf = jnp.where(lane < ROT_DIM, inv_freq[lane % ROT_HALF], 0.0).reshape(1, LANES)
    return pl.pallas_call(
        _rope_table_kernel,
        grid=(rows // tm,),
        in_specs=[pl.BlockSpec((tm, 1), lambda i: (i, 0)),
                  pl.BlockSpec((1, LANES), lambda i: (0, 0))],
        out_specs=[pl.BlockSpec((tm, LANES), lambda i: (i, 0))] * 2,
        out_shape=[jax.ShapeDtypeStruct((rows, LANES), F32)] * 2,
        compiler_params=_params("parallel"),
        name="rope_tables",
    )(positions.reshape(rows, 1), invf)


def _ffn_kernel(x_ref, g_ref, wg_ref, wu_ref, wo_ref, fg_ref, o_ref, n_ref, acc_ref, *, nj, final):
    j = pl.program_id(1)

    @pl.when(j == 0)
    def _():
        n_ref[...] = _rmsnorm(x_ref[...], g_ref[...], NORM_EPS).astype(BF16)

    n = n_ref[...]
    gate = jnp.dot(n, wg_ref[...], preferred_element_type=F32)
    up = jnp.dot(n, wu_ref[...], preferred_element_type=F32)
    h = (gate * jax.nn.sigmoid(gate) * up).astype(BF16)
    part = jnp.dot(h, wo_ref[...], preferred_element_type=F32)

    @pl.when(j == 0)
    def _():
        acc_ref[...] = part

    @pl.when(j > 0)
    def _():
        acc_ref[...] += part

    @pl.when(j == nj - 1)
    def _():
        y = x_ref[...] + 0.5 * acc_ref[...]
        if final:
            y = _rmsnorm(y, fg_ref[...], NORM_EPS)
        o_ref[...] = y


def _ffn(x, gain, w_in, w_out, final_gain, final):
    rows = x.shape[0]
    tm, tf = 512, 1408
    nj = D_FF // tf
    return pl.pallas_call(
        functools.partial(_ffn_kernel, nj=nj, final=final),
        grid=(rows // tm, nj),
        in_specs=[pl.BlockSpec((tm, D_MODEL), lambda i, j: (i, 0)),
                  pl.BlockSpec((1, D_MODEL), lambda i, j: (0, 0)),
                  pl.BlockSpec((D_MODEL, tf), lambda i, j: (0, j)),
                  pl.BlockSpec((D_MODEL, tf), lambda i, j, nj=nj: (0, j + nj)),
                  pl.BlockSpec((tf, D_MODEL), lambda i, j: (j, 0)),
                  pl.BlockSpec((1, D_MODEL), lambda i, j: (0, 0))],
        out_specs=pl.BlockSpec((tm, D_MODEL), lambda i, j: (i, 0)),
        out_shape=jax.ShapeDtypeStruct((rows, D_MODEL), F32),
        scratch_shapes=[pltpu.VMEM((tm, D_MODEL), BF16), pltpu.VMEM((tm, D_MODEL), F32)],
        compiler_params=_params("parallel", "arbitrary"),
        name="ffn_final" if final else "ffn",
    )(x, gain, w_in, w_in, w_out, final_gain)


def _proj_kernel(x_ref, g_ref, w_ref, sc_ref, *rest, rope):
    if rope:
        cos_ref, sin_ref, o_ref, n_ref = rest
    else:
        o_ref, n_ref = rest
    j = pl.program_id(1)

    @pl.when(j == 0)
    def _():
        n_ref[...] = _rmsnorm(x_ref[...], g_ref[...], NORM_EPS).astype(BF16)

    p = jnp.dot(n_ref[...], w_ref[...], preferred_element_type=F32) * sc_ref[...]
    if not rope:
        o_ref[...] = p.astype(o_ref.dtype)
        return
    cos = cos_ref[...]
    sin = sin_ref[...]
    first_half = lax.broadcasted_iota(jnp.int32, cos.shape, 1) % HEAD_DIM < ROT_HALF
    for c in range(p.shape[1] // LANES):
        t = p[:, c * LANES:(c + 1) * LANES]
        partner = jnp.where(first_half,
                            pltpu.roll(t, LANES - ROT_HALF, axis=1),
                            pltpu.roll(t, ROT_HALF, axis=1))
        o_ref[:, c * LANES:(c + 1) * LANES] = (t * cos + partner * sin).astype(o_ref.dtype)


def _proj(x, gain, w, scale, tables, tn):
    rows = x.shape[0]
    tm = 512
    cols = w.shape[1]
    rope = tables is not None
    in_specs = [pl.BlockSpec((tm, D_MODEL), lambda i, j: (i, 0)),
                pl.BlockSpec((1, D_MODEL), lambda i, j: (0, 0)),
                pl.BlockSpec((D_MODEL, tn), lambda i, j: (0, j)),
                pl.BlockSpec((1, tn), lambda i, j: (0, j))]
    args = [x, gain, w, scale]
    if rope:
        in_specs += [pl.BlockSpec((tm, LANES), lambda i, j: (i, 0))] * 2
        args += list(tables)
    return pl.pallas_call(
        functools.partial(_proj_kernel, rope=rope),
        grid=(rows // tm, cols // tn),
        in_specs=in_specs,
        out_specs=pl.BlockSpec((tm, tn), lambda i, j: (i, j)),
        out_shape=jax.ShapeDtypeStruct((rows, cols), BF16),
        scratch_shapes=[pltpu.VMEM((tm, D_MODEL), BF16)],
        compiler_params=_params("parallel", "arbitrary"),
        name="proj_rope" if rope else "proj_plain",
    )(*args)


def _nt_dot(a, b):
    return lax.dot_general(a, b, (((1,), (1,)), ((), ())), preferred_element_type=F32)


def _sb_kernel(q_ref, k_ref, v_ref, o_ref, *, tq):
    i = pl.program_id(2)
    q2 = q_ref[...]
    lane = lax.broadcasted_iota(jnp.int32, (1, LANES), 1)
    qms = [jnp.where(lane < HEAD_DIM, q2, jnp.zeros_like(q2)),
           jnp.where(lane >= HEAD_DIM, q2, jnp.zeros_like(q2))]
    row = lax.broadcasted_iota(jnp.int32, (tq, tq), 0)
    col = lax.broadcasted_iota(jnp.int32, (tq, tq), 1)
    later_key = (row > col).astype(BF16)
    strict = col < row

    def block(j, state, masked):
        start = pl.multiple_of(j * tq, tq)
        kb = k_ref[pl.ds(start, tq), :]
        vb = v_ref[pl.ds(start, tq), :]
        z = [_nt_dot(qm, kb) for qm in qms]
        lp = [jnp.log1p(jnp.exp(-jnp.abs(x))) for x in z]
        log_1m_beta = [-(jnp.maximum(x, 0.0) + y) for x, y in zip(z, lp)]
        log_beta = [jnp.minimum(x, 0.0) - y for x, y in zip(z, lp)]
        if masked:
            log_1m_beta = [jnp.where(strict, x, 0.0) for x in log_1m_beta]
        hi = [x.astype(BF16) for x in log_1m_beta]
        lo = [(x - y.astype(F32)).astype(BF16) for x, y in zip(log_1m_beta, hi)]
        between = [jnp.dot(x, later_key, preferred_element_type=F32)
                   + jnp.dot(y, later_key, preferred_element_type=F32) for x, y in zip(hi, lo)]
        w = [jnp.exp(x + y + st[1]) for x, y, st in zip(log_beta, between, state)]
        if masked:
            w = [jnp.where(strict, x, 0.0) for x in w]
        pv = [jnp.dot(x.astype(BF16), vb, preferred_element_type=F32) for x in w]
        sums = [jnp.sum(x, axis=-1, keepdims=True) for x in log_1m_beta]
        return tuple((st[0] + x, st[1] + y) for st, x, y in zip(state, pv, sums))

    def largest_carry(state):
        return jnp.max(jnp.maximum(state[0][1], state[1][1]))

    def more(c):
        j, _, top = c
        return (j >= 0) & (top > SB_NEGLIGIBLE)

    def step(c):
        j, st, _ = c
        st = block(j, st, False)
        return j - 1, st, largest_carry(st)

    zero = (jnp.zeros((tq, LANES), F32), jnp.zeros((tq, 1), F32))
    state = block(i, (zero, zero), True)
    _, state, _ = lax.while_loop(more, step, (i - 1, state, largest_carry(state)))
    o_ref[...] = jnp.where(lane < HEAD_DIM, state[0][0], state[1][0]).astype(o_ref.dtype)


def _stick_breaking(p_plain, batch, seq):
    tq = 256
    nq = seq // tq
    pairs = SB_WIDTH // LANES
    kcol, vcol = P_KA // LANES, P_VA // LANES
    return pl.pallas_call(
        functools.partial(_sb_kernel, tq=tq),
        grid=(batch, pairs, nq),
        in_specs=[pl.BlockSpec((tq, LANES), lambda b, p, i: (b * nq + i, p)),
                  pl.BlockSpec((seq, LANES), lambda b, p, i: (b, kcol + p)),
                  pl.BlockSpec((seq, LANES), lambda b, p, i: (b, vcol + p))],
        out_specs=pl.BlockSpec((tq, LANES), lambda b, p, i: (b * nq + i, p)),
        out_shape=jax.ShapeDtypeStruct((batch * seq, SB_WIDTH), BF16),
        compiler_params=_params("parallel", "parallel", "arbitrary"),
        name="stick_breaking",
    )(p_plain, p_plain, p_plain)


def _dil_kernel(q_ref, kp_ref, kc_ref, vp_ref, vc_ref, o_ref, lse_ref, *, nsub, nres):
    n = pl.program_id(2)
    span = DIL_SPAN
    gw = DIL_GROUP_WIDTH
    qi = lax.broadcasted_iota(jnp.int32, (span, 2 * span), 0) + span
    kj = lax.broadcasted_iota(jnp.int32, (span, 2 * span), 1)
    dist = qi - kj
    band = (dist >= 0) & (dist <= span)
    first_key = jnp.where(n > 0, 0, span)
    band_first = band & (kj >= first_key)
    lane = lax.broadcasted_iota(jnp.int32, (1, gw), 1)
    heads = [(lane >= h * HEAD_DIM) & (lane < (h + 1) * HEAD_DIM) for h in range(gw // HEAD_DIM)]
    tiles = [(slice(r * span, (r + 1) * span), slice(c * gw, (c + 1) * gw), r)
             for c in range(nres) for r in range(nsub)]

    def keys_of(prev_ref, cur_ref, rows, cols, r):
        if r == 0:
            return jnp.concatenate([prev_ref[:, cols], cur_ref[rows, cols]], axis=0)
        return cur_ref[(r - 1) * span:(r + 1) * span, cols]

    chains = [(t, h) for t in range(len(tiles)) for h in range(len(heads))]
    qs = [q_ref[rows, cols] for rows, cols, _ in tiles]
    ks = [keys_of(kp_ref, kc_ref, *tile) for tile in tiles]
    vs = [keys_of(vp_ref, vc_ref, *tile) for tile in tiles]
    s = [_nt_dot(jnp.where(heads[h], qs[t], jnp.zeros_like(qs[t])), ks[t]) for t, h in chains]
    s = [jnp.where(band_first if tiles[t][2] == 0 else band, x, -jnp.inf) for x, (t, h) in zip(s, chains)]
    m = [jnp.max(x, axis=-1, keepdims=True) for x in s]
    p = [jnp.exp(x - mx) for x, mx in zip(s, m)]
    l = [jnp.sum(x, axis=-1, keepdims=True) for x in p]
    o = [jnp.dot(x.astype(BF16), vs[t], preferred_element_type=F32) for x, (t, h) in zip(p, chains)]
    o = [x * (1.0 / lx) for x, lx in zip(o, l)]
    lse = [mx + jnp.log(lx) for mx, lx in zip(m, l)]
    for t, (rows, cols, _) in enumerate(tiles):
        out = jnp.zeros((span, gw), F32)
        lse_out = jnp.zeros((span, gw), F32)
        for h in range(len(heads)):
            out = jnp.where(heads[h], o[t * len(heads) + h], out)
            lse_out = jnp.where(heads[h], lse[t * len(heads) + h], lse_out)
        o_ref[rows, cols] = out
        lse_ref[rows, cols] = lse_out


def _dilated_group(q, k, v, batch, seq, dil):
    span = DIL_SPAN
    sub = seq // dil
    gw = DIL_GROUP_WIDTH
    nsub = min(4, sub // span)
    nres = min(dil, 4 // nsub)
    spans_per_seq = sub // span
    nb = spans_per_seq // nsub
    regroup = lambda t: t.reshape(batch * sub, dil * gw)
    cur = pl.BlockSpec((nsub * span, nres * gw), lambda b, c, n: (b * nb + n, c))
    prev = pl.BlockSpec((span, nres * gw),
                        lambda b, c, n: (b * spans_per_seq + jnp.maximum(n * nsub - 1, 0), c))
    o, lse = pl.pallas_call(
        functools.partial(_dil_kernel, nsub=nsub, nres=nres),
        grid=(batch, dil // nres, nb),
        in_specs=[cur, prev, cur, prev, cur],
        out_specs=[cur, cur],
        out_shape=[jax.ShapeDtypeStruct((batch * sub, dil * gw), F32)] * 2,
        compiler_params=_params("parallel", "parallel", "arbitrary"),
        name=f"dilated_d{dil}",
    )(regroup(q), regroup(k), regroup(k), regroup(v), regroup(v))
    return o.reshape(batch * seq, gw), lse.reshape(batch * seq, gw)


def _diff_kernel(q1_ref, q2_ref, k1_ref, k2_ref, v_ref, lq1_ref, lk1_ref, lq2_ref, lk2_ref, g_ref,
                 o_ref, *, tq, lam_init):
    i = pl.program_id(2)
    lane = lax.broadcasted_iota(jnp.int32, (1, LANES), 1)
    halves = (lane < HEAD_DIM, lane >= HEAD_DIM)
    q_pairs = (q1_ref[...], q2_ref[...])
    qs = [[jnp.where(half, q, jnp.zeros_like(q)) for q in q_pairs] for half in halves]
    k_refs = (k1_ref, k2_ref)
    row = lax.broadcasted_iota(jnp.int32, (tq, tq), 0)
    col = lax.broadcasted_iota(jnp.int32, (tq, tq), 1)
    causal = col <= row

    def block(j, state, masked):
        start = pl.multiple_of(j * tq, tq)
        kbs = [r[pl.ds(start, tq), :] for r in k_refs]
        vbs = [v_ref[pl.ds(start, tq), hh * DIFF_V_DIM:(hh + 1) * DIFF_V_DIM] for hh in range(2)]
        chains = [(hh, t) for hh in range(2) for t in range(2)]
        s = [_nt_dot(qs[hh][t], kbs[t]) for hh, t in chains]
        if masked:
            s = [jnp.where(causal, x, -jnp.inf) for x in s]
        m_new = [jnp.maximum(st[0], jnp.max(x, axis=-1, keepdims=True)) for st, x in zip(state, s)]
        p = [jnp.exp(x - mx) for x, mx in zip(s, m_new)]
        pv = [jnp.dot(x.astype(BF16), vbs[hh], preferred_element_type=F32) for x, (hh, t) in zip(p, chains)]
        alpha = [jnp.exp(st[0] - mx) for st, mx in zip(state, m_new)]
        l_new = [a * st[1] + jnp.sum(x, axis=-1, keepdims=True) for a, st, x in zip(alpha, state, p)]
        acc = [a * st[2] + x for a, st, x in zip(alpha, state, pv)]
        return tuple(zip(m_new, l_new, acc))

    init = (jnp.full((tq, 1), -jnp.inf, F32), jnp.zeros((tq, 1), F32), jnp.zeros((tq, DIFF_V_DIM), F32))
    state = block(i, (init,) * 4, True)
    state = lax.fori_loop(0, i, lambda jj, st: block(i - 1 - jj, st, False), state)

    lam = (jnp.exp(jnp.sum(lq1_ref[...] * lk1_ref[...], axis=-1, keepdims=True))
           - jnp.exp(jnp.sum(lq2_ref[...] * lk2_ref[...], axis=-1, keepdims=True)) + lam_init)
    for hh in range(2):
        (_, l1, a1), (_, l2, a2) = state[2 * hh], state[2 * hh + 1]
        o = a1 / l1 - lam * (a2 / l2)
        o_ref[:, hh * DIFF_V_DIM:(hh + 1) * DIFF_V_DIM] = (
            _rmsnorm(o, g_ref[...], SUBLN_EPS) * (1.0 - lam_init)).astype(o_ref.dtype)


def _diff_attention(p_rope, p_plain, lam_vecs, subln, batch, seq, lam_init):
    tq = 256
    nq = seq // tq
    pair_w = 2 * DIFF_V_DIM
    q1c, q2c, k1c, k2c = (c // LANES for c in (R_Q1, R_Q2, R_K1, R_K2))
    vcol = P_VC // pair_w
    qspec = lambda c0: pl.BlockSpec((tq, LANES), lambda b, p, i: (b * nq + i, c0 + p))
    kspec = lambda c0: pl.BlockSpec((seq, LANES), lambda b, p, i: (b, c0 + p))
    vec = lambda w: pl.BlockSpec((1, w), lambda b, p, i: (0, 0))
    return pl.pallas_call(
        functools.partial(_diff_kernel, tq=tq, lam_init=lam_init),
        grid=(batch, DIFF_HEADS // 2, nq),
        in_specs=[qspec(q1c), qspec(q2c), kspec(k1c), kspec(k2c),
                  pl.BlockSpec((seq, pair_w), lambda b, p, i: (b, vcol + p)),
                  vec(HEAD_DIM), vec(HEAD_DIM), vec(HEAD_DIM), vec(HEAD_DIM), vec(DIFF_V_DIM)],
        out_specs=pl.BlockSpec((tq, pair_w), lambda b, p, i: (b * nq + i, p)),
        out_shape=jax.ShapeDtypeStruct((batch * seq, DIFF_V_WIDTH), BF16),
        compiler_params=_params("parallel", "parallel", "arbitrary"),
        name="diff_attention",
    )(p_rope, p_rope, p_rope, p_rope, p_plain, *lam_vecs, subln)


def _mix_kernel(x_ref, g_ref, wg_ref, bg_ref, ya_ref, yc_ref,
                o1_ref, o2_ref, o3_ref, s1_ref, s2_ref, s3_ref,
                wa_ref, wb_ref, wc_ref, wo_ref, out_ref):
    x = x_ref[...]
    hn = _rmsnorm(x, g_ref[...], NORM_EPS).astype(BF16)
    gates = jax.nn.sigmoid(jnp.dot(hn, wg_ref[...], preferred_element_type=F32) + bg_ref[...])
    lses = [s1_ref[...], s2_ref[...], s3_ref[...]]
    m = jnp.maximum(jnp.maximum(lses[0], lses[1]), lses[2])
    es = [jnp.exp(s - m) for s in lses]
    den = es[0] + es[1] + es[2]
    yb = (es[0] / den) * o1_ref[...] + (es[1] / den) * o2_ref[...] + (es[2] / den) * o3_ref[...]
    up_a = jnp.dot(ya_ref[...], wa_ref[...], preferred_element_type=F32)
    up_b = jnp.dot(yb.astype(BF16), wb_ref[...], preferred_element_type=F32)
    up_c = jnp.dot(yc_ref[...], wc_ref[...], preferred_element_type=F32)
    d = D_MODEL
    merged = gates[:, :d] * up_a + gates[:, d:2 * d] * up_b + gates[:, 2 * d:] * up_c
    out_ref[...] = x + jnp.dot(merged.astype(BF16), wo_ref[...], preferred_element_type=F32)


def _mix_out(x, gain, w_gate, b_gate, y_a, y_c, dil_o, dil_lse, w_up_a, w_up_b, w_up_c, w_out):
    rows = x.shape[0]
    tm = 256
    row = lambda w: pl.BlockSpec((tm, w), lambda i: (i, 0))
    full = lambda a: pl.BlockSpec(a.shape, lambda i: (0, 0))
    gw = DIL_GROUP_WIDTH
    return pl.pallas_call(
        _mix_kernel,
        grid=(rows // tm,),
        in_specs=[row(D_MODEL), full(gain), full(w_gate), full(b_gate), row(SB_WIDTH), row(DIFF_V_WIDTH),
                  row(gw), row(gw), row(gw), row(gw), row(gw), row(gw),
                  full(w_up_a), full(w_up_b), full(w_up_c), full(w_out)],
        out_specs=row(D_MODEL),
        out_shape=jax.ShapeDtypeStruct((rows, D_MODEL), F32),
        compiler_params=_params("parallel"),
        name="mix_out",
    )(x, gain, w_gate, b_gate, y_a, y_c, *dil_o, *dil_lse, w_up_a, w_up_b, w_up_c, w_out)


def _column_scale(width, scaled_ranges):
    col = jnp.arange(width)
    s = jnp.ones((width,), F32)
    for lo, hi in scaled_ranges:
        s = jnp.where((col >= lo) & (col < hi), QK_SCALE, s)
    return s.reshape(1, width)


def kernel(x, positions, ffn1_norm, ffn1_w_in, ffn1_w_out, mix_norm, w_in, b_gate, lam_q1, lam_k1, lam_q2, lam_k2, diff_subln, w_up_a, w_up_b, w_up_c, w_out, ffn2_norm, ffn2_w_in, ffn2_w_out, final_norm):
    batch, seq, d = x.shape
    rows = batch * seq
    x = x.reshape(rows, d)
    tables = _rope_tables(positions)
    vec = lambda t: t.reshape(1, -1)

    o_qa, o_ka, o_va = 0, SB_WIDTH, 2 * SB_WIDTH
    o_qb = 3 * SB_WIDTH
    o_kb, o_vb = o_qb + DIL_WIDTH, o_qb + 2 * DIL_WIDTH
    o_q1 = o_qb + 3 * DIL_WIDTH
    o_vc = o_q1 + 4 * DIFF_QK_WIDTH
    o_gate = o_vc + DIFF_V_WIDTH
    scale_rope = _column_scale(ROPE_COLS, [(R_QB, R_QB + DIL_WIDTH), (R_Q1, R_K1)])
    scale_plain = _column_scale(PLAIN_COLS, [(P_QA, P_QA + SB_WIDTH)])

    for l in range(DEPTH):
        wl = w_in[l]
        w_rope = jnp.concatenate([wl[:, o_qb:o_vb], wl[:, o_q1:o_vc]], axis=1).astype(BF16)
        w_plain = jnp.concatenate([wl[:, o_qa:o_qb], wl[:, o_vb:o_q1], wl[:, o_vc:o_gate]], axis=1).astype(BF16)
        w_gate = wl[:, o_gate:].astype(BF16)

        x = _ffn(x, vec(ffn1_norm[l]), ffn1_w_in[l].astype(BF16), ffn1_w_out[l].astype(BF16),
                 vec(final_norm), False)

        gain = vec(mix_norm[l])
        p_rope = _proj(x, gain, w_rope, scale_rope, tables, ROPE_COLS // 2)
        p_plain = _proj(x, gain, w_plain, scale_plain, None, PLAIN_COLS // 2)

        y_a = _stick_breaking(p_plain, batch, seq)

        dil_o, dil_lse = [], []
        for g, (window, dil) in enumerate(DIL_PATTERNS):
            assert window // dil == DIL_SPAN
            c = g * DIL_GROUP_WIDTH
            o, lse = _dilated_group(p_rope[:, R_QB + c:R_QB + c + DIL_GROUP_WIDTH],
                                    p_rope[:, R_KB + c:R_KB + c + DIL_GROUP_WIDTH],
                                    p_plain[:, P_VB + c:P_VB + c + DIL_GROUP_WIDTH],
                                    batch, seq, dil)
            dil_o.append(o)
            dil_lse.append(lse)

        lam_init = 0.8 - 0.6 * math.exp(-0.3 * l)
        y_c = _diff_attention(p_rope, p_plain,
                              [vec(t[l]) for t in (lam_q1, lam_k1, lam_q2, lam_k2)],
                              vec(diff_subln[l]), batch, seq, lam_init)

        x = _mix_out(x, gain, w_gate, vec(b_gate[l]), y_a, y_c, dil_o, dil_lse,
                     w_up_a[l].astype(BF16), w_up_b[l].astype(BF16), w_up_c[l].astype(BF16),
                     w_out[l].astype(BF16))

        x = _ffn(x, vec(ffn2_norm[l]), ffn2_w_in[l].astype(BF16), ffn2_w_out[l].astype(BF16),
                 vec(final_norm), l == DEPTH - 1)
    return x.reshape(batch, seq, d)
```

```python
import functools
import math

import jax
import jax.numpy as jnp
from jax import lax
from jax.experimental import pallas as pl
from jax.experimental.pallas import tpu as pltpu

F32 = jnp.float32
BF16 = jnp.bfloat16

D_MODEL = 1024
DEPTH = 2
HEAD_DIM = 64
ROT_DIM = HEAD_DIM // 4
ROT_HALF = ROT_DIM // 2
ROPE_THETA = 500000.0
NORM_EPS = 1e-6
SUBLN_EPS = 1e-5
QK_SCALE = HEAD_DIM ** -0.5

SB_HEADS = 8
SB_WIDTH = SB_HEADS * HEAD_DIM
DIL_PATTERNS = ((128, 1), (512, 4), (2048, 16))
DIL_SPAN = 128
DIL_GROUP_WIDTH = 4 * HEAD_DIM
DIL_WIDTH = 3 * DIL_GROUP_WIDTH
DIFF_HEADS = 4
DIFF_QK_WIDTH = DIFF_HEADS * HEAD_DIM
DIFF_V_DIM = 2 * HEAD_DIM
DIFF_V_WIDTH = DIFF_HEADS * DIFF_V_DIM
GATE_WIDTH = 3 * D_MODEL
D_FF = 2816

SB_NEGLIGIBLE = -104.0
LANES = 128
VMEM_LIMIT = 56 * 1024 * 1024

ROTARY_COLS = 2 * DIL_WIDTH + 4 * DIFF_QK_WIDTH
QKV_COLS = ROTARY_COLS + DIL_WIDTH + 3 * SB_WIDTH + DIFF_V_WIDTH
DIL_COLS = 3 * DIL_WIDTH
D_QB, D_KB, D_VB = 0, DIL_WIDTH, 2 * DIL_WIDTH
ROPE_COLS = 4 * DIFF_QK_WIDTH
R_Q1, R_Q2, R_K1, R_K2 = (t * DIFF_QK_WIDTH for t in range(4))
PLAIN_COLS = 3 * SB_WIDTH + DIFF_V_WIDTH
P_QA, P_KA, P_VA, P_VC = 0, SB_WIDTH, 2 * SB_WIDTH, 3 * SB_WIDTH


def _params(*sem):
    return pltpu.CompilerParams(dimension_semantics=sem, vmem_limit_bytes=VMEM_LIMIT)


def _rmsnorm(x, g, eps):
    ms = jnp.mean(x * x, axis=-1, keepdims=True)
    return x * lax.rsqrt(ms + eps) * g


def _rope_table_kernel(pos_ref, invf_ref, cos_ref, sin_ref):
    ang = pos_ref[...].astype(F32) * invf_ref[...]
    lane = lax.broadcasted_iota(jnp.int32, ang.shape, 1) % HEAD_DIM
    c = jnp.cos(ang)
    s = jnp.sin(ang)
    cos_ref[...] = jnp.where(lane < ROT_DIM, c, 1.0)
    sin_ref[...] = jnp.where(lane < ROT_HALF, -s, jnp.where(lane < ROT_DIM, s, 0.0))


def _rope_tables(positions):
    rows = positions.size
    tm = 1024
    inv_freq = ROPE_THETA ** (-jnp.arange(0, ROT_DIM, 2, dtype=F32) / ROT_DIM)
    lane = jnp.arange(LANES) % HEAD_DIM
    invf = jnp.where(lane < ROT_DIM, inv_freq[lane % ROT_HALF], 0.0).reshape(1, LANES)
    return pl.pallas_call(
        _rope_table_kernel,
        grid=(rows // tm,),
        in_specs=[pl.BlockSpec((tm, 1), lambda i: (i, 0)),
                  pl.BlockSpec((1, LANES), lambda i: (0, 0))],
        out_specs=[pl.BlockSpec((tm, LANES), lambda i: (i, 0))] * 2,
        out_shape=[jax.ShapeDtypeStruct((rows, LANES), F32)] * 2,
        compiler_params=_params("parallel"),
        name="rope_tables",
    )(positions.reshape(rows, 1), invf)


def _ffn_kernel(x_ref, g_ref, wg_ref, wu_ref, wo_ref, fg_ref, o_ref, *, final):
    x = x_ref[...]
    n = _rmsnorm(x, g_ref[...], NORM_EPS).astype(BF16)
    gate = jnp.dot(n, wg_ref[...], preferred_element_type=F32)
    up = jnp.dot(n, wu_ref[...], preferred_element_type=F32)
    h = (gate * jax.nn.sigmoid(gate) * up).astype(BF16)
    y = x + 0.5 * jnp.dot(h, wo_ref[...], preferred_element_type=F32)
    if final:
        y = _rmsnorm(y, fg_ref[...], NORM_EPS)
    o_ref[...] = y


def _resident(shape, index_map):
    return pl.BlockSpec(shape, index_map, pipeline_mode=pl.Buffered(1))


def _ffn(x, gain, w_in, w_out, final_gain, final):
    rows = x.shape[0]
    tm = 512
    return pl.pallas_call(
        functools.partial(_ffn_kernel, final=final),
        grid=(rows // tm,),
        in_specs=[pl.BlockSpec((tm, D_MODEL), lambda i: (i, 0)),
                  _resident((1, D_MODEL), lambda i: (0, 0)),
                  _resident((D_MODEL, D_FF), lambda i: (0, 0)),
                  _resident((D_MODEL, D_FF), lambda i: (0, 1)),
                  _resident((D_FF, D_MODEL), lambda i: (0, 0)),
                  _resident((1, D_MODEL), lambda i: (0, 0))],
        out_specs=pl.BlockSpec((tm, D_MODEL), lambda i: (i, 0)),
        out_shape=jax.ShapeDtypeStruct((rows, D_MODEL), F32),
        compiler_params=_params("parallel"),
        name="ffn_final" if final else "ffn",
    )(x, gain, w_in, w_in, w_out, final_gain)


def _proj_kernel(x_ref, g_ref, w_ref, sc_ref, cos_ref, sin_ref, dil_ref, rope_ref, plain_ref):
    n = _rmsnorm(x_ref[...], g_ref[...], NORM_EPS).astype(BF16)
    rot = jnp.dot(n, w_ref[:, :ROTARY_COLS], preferred_element_type=F32) * sc_ref[:, :ROTARY_COLS]
    rest = jnp.dot(n, w_ref[:, ROTARY_COLS:], preferred_element_type=F32) * sc_ref[:, ROTARY_COLS:]
    cos = cos_ref[...]
    sin = sin_ref[...]
    first_half = lax.broadcasted_iota(jnp.int32, cos.shape, 1) % HEAD_DIM < ROT_HALF
    for c in range(ROTARY_COLS // LANES):
        t = rot[:, c * LANES:(c + 1) * LANES]
        partner = jnp.where(first_half,
                            pltpu.roll(t, LANES - ROT_HALF, axis=1),
                            pltpu.roll(t, ROT_HALF, axis=1))
        roped = t * cos + partner * sin
        lo = c * LANES
        if lo < D_VB:
            dil_ref[:, lo:lo + LANES] = roped
        else:
            rope_ref[:, lo - D_VB:lo - D_VB + LANES] = roped.astype(rope_ref.dtype)
    dil_ref[:, D_VB:] = rest[:, :DIL_WIDTH]
    plain_ref[...] = rest[:, DIL_WIDTH:].astype(plain_ref.dtype)


def _proj(x, gain, w, scale, tables):
    rows = x.shape[0]
    tm = 512
    row = lambda width: pl.BlockSpec((tm, width), lambda i: (i, 0))
    const = lambda a: _resident(a.shape, lambda i: (0, 0))
    return pl.pallas_call(
        _proj_kernel,
        grid=(rows // tm,),
        in_specs=[row(D_MODEL), const(gain), const(w), const(scale), row(LANES), row(LANES)],
        out_specs=[row(DIL_COLS), row(ROPE_COLS), row(PLAIN_COLS)],
        out_shape=[jax.ShapeDtypeStruct((rows, DIL_COLS), F32),
                   jax.ShapeDtypeStruct((rows, ROPE_COLS), BF16),
                   jax.ShapeDtypeStruct((rows, PLAIN_COLS), BF16)],
        compiler_params=_params("parallel"),
        name="proj",
    )(x, gain, w, scale, *tables)


def _nt_dot(a, b):
    return lax.dot_general(a, b, (((1,), (1,)), ((), ())), preferred_element_type=F32)


def _sb_kernel(q_ref, k_ref, v_ref, o_ref, *, tq):
    i = pl.program_id(2)
    q2 = q_ref[...]
    lane = lax.broadcasted_iota(jnp.int32, (1, LANES), 1)
    qms = [jnp.where(lane < HEAD_DIM, q2, jnp.zeros_like(q2)),
           jnp.where(lane >= HEAD_DIM, q2, jnp.zeros_like(q2))]
    row = lax.broadcasted_iota(jnp.int32, (tq, tq), 0)
    col = lax.broadcasted_iota(jnp.int32, (tq, tq), 1)
    later_key = (row > col).astype(BF16)
    strict = col < row

    def block(j, state, masked):
        start = pl.multiple_of(j * tq, tq)
        kb = k_ref[pl.ds(start, tq), :]
        vb = v_ref[pl.ds(start, tq), :]
        z = [_nt_dot(qm, kb) for qm in qms]
        lp = [jnp.log1p(jnp.exp(-jnp.abs(x))) for x in z]
        log_1m_beta = [-(jnp.maximum(x, 0.0) + y) for x, y in zip(z, lp)]
        log_beta = [jnp.minimum(x, 0.0) - y for x, y in zip(z, lp)]
        if masked:
            log_1m_beta = [jnp.where(strict, x, 0.0) for x in log_1m_beta]
        hi = [x.astype(BF16) for x in log_1m_beta]
        lo = [(x - y.astype(F32)).astype(BF16) for x, y in zip(log_1m_beta, hi)]
        between = [jnp.dot(x, later_key, preferred_element_type=F32)
                   + jnp.dot(y, later_key, preferred_element_type=F32) for x, y in zip(hi, lo)]
        w = [jnp.exp(x + y + st[1]) for x, y, st in zip(log_beta, between, state)]
        if masked:
            w = [jnp.where(strict, x, 0.0) for x in w]
        pv = [jnp.dot(x.astype(BF16), vb, preferred_element_type=F32) for x in w]
        sums = [jnp.sum(x, axis=-1, keepdims=True) for x in log_1m_beta]
        return tuple((st[0] + x, st[1] + y) for st, x, y in zip(state, pv, sums))

    def largest_carry(state):
        return jnp.max(jnp.maximum(state[0][1], state[1][1]))

    def more(c):
        j, _, top = c
        return (j >= 0) & (top > SB_NEGLIGIBLE)

    def step(c):
        j, st, _ = c
        st = block(j, st, False)
        return j - 1, st, largest_carry(st)

    zero = (jnp.zeros((tq, LANES), F32), jnp.zeros((tq, 1), F32))
    state = block(i, (zero, zero), True)
    _, state, _ = lax.while_loop(more, step, (i - 1, state, largest_carry(state)))
    o_ref[...] = jnp.where(lane < HEAD_DIM, state[0][0], state[1][0]).astype(o_ref.dtype)


def _stick_breaking(p_plain, batch, seq):
    tq = 256
    nq = seq // tq
    pairs = SB_WIDTH // LANES
    kcol, vcol = P_KA // LANES, P_VA // LANES
    return pl.pallas_call(
        functools.partial(_sb_kernel, tq=tq),
        grid=(batch, pairs, nq),
        in_specs=[pl.BlockSpec((tq, LANES), lambda b, p, i: (b * nq + i, p)),
                  pl.BlockSpec((seq, LANES), lambda b, p, i: (b, kcol + p)),
                  pl.BlockSpec((seq, LANES), lambda b, p, i: (b, vcol + p))],
        out_specs=pl.BlockSpec((tq, LANES), lambda b, p, i: (b * nq + i, p)),
        out_shape=jax.ShapeDtypeStruct((batch * seq, SB_WIDTH), BF16),
        compiler_params=_params("parallel", "parallel", "arbitrary"),
        name="stick_breaking",
    )(p_plain, p_plain, p_plain)


def _rows_of(start, size, stride):
    return pl.ds(start, size) if stride == 1 else pl.ds(start, size, stride=stride)


def _dil_kernel(*refs, dil, nsub, has_prev):
    if has_prev:
        q_ref, kc_ref, vc_ref, kp_ref, vp_ref, o_ref, lse_ref = refs
    else:
        q_ref, kc_ref, vc_ref, o_ref, lse_ref = refs
        kp_ref = vp_ref = None
    n = pl.program_id(2)
    span = DIL_SPAN
    gw = LANES
    nkeys = 2 * span if has_prev else span
    qi = lax.broadcasted_iota(jnp.int32, (span, nkeys), 0) + (nkeys - span)
    kj = lax.broadcasted_iota(jnp.int32, (span, nkeys), 1)
    dist = qi - kj
    band = (dist >= 0) & (dist <= span)
    if has_prev:
        first_key = jnp.where(n > 0, 0, span)
        band_first = band & (kj >= first_key)
    lane = lax.broadcasted_iota(jnp.int32, (1, gw), 1)
    heads = [(lane >= h * HEAD_DIM) & (lane < (h + 1) * HEAD_DIM) for h in range(gw // HEAD_DIM)]

    def tile_rows(r, c):
        return _rows_of(r * span * dil + c, span, dil)

    def keys_of(cur_ref, prev_ref, r, c):
        own = cur_ref[tile_rows(r, c), :]
        if not has_prev:
            return own.astype(BF16)
        before = cur_ref[tile_rows(r - 1, c), :] if r > 0 else prev_ref[_rows_of(c, span, dil), :]
        return jnp.concatenate([before, own], axis=0).astype(BF16)

    all_tiles = [(r, c) for r in range(nsub) for c in range(dil)]
    per_pass = 8
    for t0 in range(0, len(all_tiles), per_pass):
        tiles = all_tiles[t0:t0 + per_pass]
        chains = [(t, h) for t in range(len(tiles)) for h in range(len(heads))]
        qs = [q_ref[tile_rows(r, c), :].astype(BF16) for r, c in tiles]
        ks = [keys_of(kc_ref, kp_ref, r, c) for r, c in tiles]
        vs = [keys_of(vc_ref, vp_ref, r, c) for r, c in tiles]
        s = [_nt_dot(jnp.where(heads[h], qs[t], jnp.zeros_like(qs[t])), ks[t]) for t, h in chains]
        s = [jnp.where(band_first if has_prev and tiles[t][0] == 0 else band, x, -jnp.inf)
             for x, (t, h) in zip(s, chains)]
        m = [jnp.max(x, axis=-1, keepdims=True) for x in s]
        p = [jnp.exp(x - mx) for x, mx in zip(s, m)]
        l = [jnp.sum(x, axis=-1, keepdims=True) for x in p]
        o = [jnp.dot(x.astype(BF16), vs[t], preferred_element_type=F32) for x, (t, h) in zip(p, chains)]
        o = [x * (1.0 / lx) for x, lx in zip(o, l)]
        lse = [mx + jnp.log(lx) for mx, lx in zip(m, l)]
        for t, (r, c) in enumerate(tiles):
            out = jnp.zeros((span, gw), F32)
            lse_out = jnp.zeros((span, gw), F32)
            for h in range(len(heads)):
                out = jnp.where(heads[h], o[t * len(heads) + h], out)
                lse_out = jnp.where(heads[h], lse[t * len(heads) + h], lse_out)
            o_ref[tile_rows(r, c), :] = out
            lse_ref[tile_rows(r, c), :] = lse_out


def _dilated_group(p_dil, group, batch, seq, dil):
    span = DIL_SPAN
    pairs = DIL_GROUP_WIDTH // LANES
    nsub = max(1, 8 // dil)
    tb = nsub * span * dil
    pb = span * dil
    has_prev = tb < seq
    nb = seq // tb
    qcol, kcol, vcol = (c // LANES + group * pairs for c in (D_QB, D_KB, D_VB))
    cur = lambda col: pl.BlockSpec((tb, LANES), lambda b, u, n: (b * nb + n, col + u))
    prev = lambda col: pl.BlockSpec(
        (pb, LANES), lambda b, u, n: (b * (seq // pb) + jnp.maximum(n * (tb // pb) - 1, 0), col + u))
    in_specs = [cur(qcol), cur(kcol), cur(vcol)] + ([prev(kcol), prev(vcol)] if has_prev else [])
    return pl.pallas_call(
        functools.partial(_dil_kernel, dil=dil, nsub=nsub, has_prev=has_prev),
        grid=(batch, pairs, nb),
        in_specs=in_specs,
        out_specs=[cur(0), cur(0)],
        out_shape=[jax.ShapeDtypeStruct((batch * seq, DIL_GROUP_WIDTH), F32)] * 2,
        compiler_params=_params("parallel", "parallel", "arbitrary"),
        name=f"dilated_d{dil}",
    )(*([p_dil] * len(in_specs)))


def _diff_kernel(q1_ref, q2_ref, k1_ref, k2_ref, v_ref, lq1_ref, lk1_ref, lq2_ref, lk2_ref, g_ref,
                 o_ref, *, tq, lam_init):
    i = pl.program_id(2)
    lane = lax.broadcasted_iota(jnp.int32, (1, LANES), 1)
    halves = (lane < HEAD_DIM, lane >= HEAD_DIM)
    q_pairs = (q1_ref[...], q2_ref[...])
    qs = [[jnp.where(half, q, jnp.zeros_like(q)) for q in q_pairs] for half in halves]
    k_refs = (k1_ref, k2_ref)
    row = lax.broadcasted_iota(jnp.int32, (tq, tq), 0)
    col = lax.broadcasted_iota(jnp.int32, (tq, tq), 1)
    causal = col <= row

    def block(j, state, masked):
        start = pl.multiple_of(j * tq, tq)
        kbs = [r[pl.ds(start, tq), :] for r in k_refs]
        vbs = [v_ref[pl.ds(start, tq), hh * DIFF_V_DIM:(hh + 1) * DIFF_V_DIM] for hh in range(2)]
        chains = [(hh, t) for hh in range(2) for t in range(2)]
        s = [_nt_dot(qs[hh][t], kbs[t]) for hh, t in chains]
        if masked:
            s = [jnp.where(causal, x, -jnp.inf) for x in s]
        m_new = [jnp.maximum(st[0], jnp.max(x, axis=-1, keepdims=True)) for st, x in zip(state, s)]
        p = [jnp.exp(x - mx) for x, mx in zip(s, m_new)]
        pv = [jnp.dot(x.astype(BF16), vbs[hh], preferred_element_type=F32) for x, (hh, t) in zip(p, chains)]
        alpha = [jnp.exp(st[0] - mx) for st, mx in zip(state, m_new)]
        l_new = [a * st[1] + jnp.sum(x, axis=-1, keepdims=True) for a, st, x in zip(alpha, state, p)]
        acc = [a * st[2] + x for a, st, x in zip(alpha, state, pv)]
        return tuple(zip(m_new, l_new, acc))

    init = (jnp.full((tq, 1), -jnp.inf, F32), jnp.zeros((tq, 1), F32), jnp.zeros((tq, DIFF_V_DIM), F32))
    state = block(i, (init,) * 4, True)
    state = lax.fori_loop(0, i, lambda jj, st: block(i - 1 - jj, st, False), state)

    lam = (jnp.exp(jnp.sum(lq1_ref[...] * lk1_ref[...], axis=-1, keepdims=True))
           - jnp.exp(jnp.sum(lq2_ref[...] * lk2_ref[...], axis=-1, keepdims=True)) + lam_init)
    for hh in range(2):
        (_, l1, a1), (_, l2, a2) = state[2 * hh], state[2 * hh + 1]
        o = a1 / l1 - lam * (a2 / l2)
        o_ref[:, hh * DIFF_V_DIM:(hh + 1) * DIFF_V_DIM] = (
            _rmsnorm(o, g_ref[...], SUBLN_EPS) * (1.0 - lam_init)).astype(o_ref.dtype)


def _diff_attention(p_rope, p_plain, lam_vecs, subln, batch, seq, lam_init):
    tq = 256
    nq = seq // tq
    pair_w = 2 * DIFF_V_DIM
    q1c, q2c, k1c, k2c = (c // LANES for c in (R_Q1, R_Q2, R_K1, R_K2))
    vcol = P_VC // pair_w
    qspec = lambda c0: pl.BlockSpec((tq, LANES), lambda b, p, i: (b * nq + i, c0 + p))
    kspec = lambda c0: pl.BlockSpec((seq, LANES), lambda b, p, i: (b, c0 + p))
    vec = lambda w: pl.BlockSpec((1, w), lambda b, p, i: (0, 0))
    return pl.pallas_call(
        functools.partial(_diff_kernel, tq=tq, lam_init=lam_init),
        grid=(batch, DIFF_HEADS // 2, nq),
        in_specs=[qspec(q1c), qspec(q2c), kspec(k1c), kspec(k2c),
                  pl.BlockSpec((seq, pair_w), lambda b, p, i: (b, vcol + p)),
                  vec(HEAD_DIM), vec(HEAD_DIM), vec(HEAD_DIM), vec(HEAD_DIM), vec(DIFF_V_DIM)],
        out_specs=pl.BlockSpec((tq, pair_w), lambda b, p, i: (b * nq + i, p)),
        out_shape=jax.ShapeDtypeStruct((batch * seq, DIFF_V_WIDTH), BF16),
        compiler_params=_params("parallel", "parallel", "arbitrary"),
        name="diff_attention",
    )(p_rope, p_rope, p_rope, p_rope, p_plain, *lam_vecs, subln)


def _mix_kernel(x_ref, g_ref, wg_ref, bg_ref, ya_ref, yc_ref,
                o1_ref, o2_ref, o3_ref, s1_ref, s2_ref, s3_ref,
                wa_ref, wb_ref, wc_ref, wo_ref, out_ref):
    x = x_ref[...]
    hn = _rmsnorm(x, g_ref[...], NORM_EPS).astype(BF16)
    gates = jax.nn.sigmoid(jnp.dot(hn, wg_ref[...], preferred_element_type=F32) + bg_ref[...])
    lses = [s1_ref[...], s2_ref[...], s3_ref[...]]
    m = jnp.maximum(jnp.maximum(lses[0], lses[1]), lses[2])
    es = [jnp.exp(s - m) for s in lses]
    den = es[0] + es[1] + es[2]
    yb = (es[0] / den) * o1_ref[...] + (es[1] / den) * o2_ref[...] + (es[2] / den) * o3_ref[...]
    up_a = jnp.dot(ya_ref[...], wa_ref[...], preferred_element_type=F32)
    up_b = jnp.dot(yb.astype(BF16), wb_ref[...], preferred_element_type=F32)
    up_c = jnp.dot(yc_ref[...], wc_ref[...], preferred_element_type=F32)
    d = D_MODEL
    merged = gates[:, :d] * up_a + gates[:, d:2 * d] * up_b + gates[:, 2 * d:] * up_c
    out_ref[...] = x + jnp.dot(merged.astype(BF16), wo_ref[...], preferred_element_type=F32)


def _mix_out(x, gain, w_gate, b_gate, y_a, y_c, dil_o, dil_lse, w_up_a, w_up_b, w_up_c, w_out):
    rows = x.shape[0]
    tm = 512
    row = lambda w: pl.BlockSpec((tm, w), lambda i: (i, 0))
    full = lambda a: _resident(a.shape, lambda i: (0, 0))
    gw = DIL_GROUP_WIDTH
    return pl.pallas_call(
        _mix_kernel,
        grid=(rows // tm,),
        in_specs=[row(D_MODEL), full(gain), full(w_gate), full(b_gate), row(SB_WIDTH), row(DIFF_V_WIDTH),
                  row(gw), row(gw), row(gw), row(gw), row(gw), row(gw),
                  full(w_up_a), full(w_up_b), full(w_up_c), full(w_out)],
        out_specs=row(D_MODEL),
        out_shape=jax.ShapeDtypeStruct((rows, D_MODEL), F32),
        compiler_params=_params("parallel"),
        name="mix_out",
    )(x, gain, w_gate, b_gate, y_a, y_c, *dil_o, *dil_lse, w_up_a, w_up_b, w_up_c, w_out)


def _column_scale(width, scaled_ranges):
    col = jnp.arange(width)
    s = jnp.ones((width,), F32)
    for lo, hi in scaled_ranges:
        s = jnp.where((col >= lo) & (col < hi), QK_SCALE, s)
    return s.reshape(1, width)


def kernel(x, positions, ffn1_norm, ffn1_w_in, ffn1_w_out, mix_norm, w_in, b_gate, lam_q1, lam_k1, lam_q2, lam_k2, diff_subln, w_up_a, w_up_b, w_up_c, w_out, ffn2_norm, ffn2_w_in, ffn2_w_out, final_norm):
    batch, seq, d = x.shape
    rows = batch * seq
    x = x.reshape(rows, d)
    tables = _rope_tables(positions)
    vec = lambda t: t.reshape(1, -1)

    o_qa, o_ka, o_va = 0, SB_WIDTH, 2 * SB_WIDTH
    o_qb = 3 * SB_WIDTH
    o_kb, o_vb = o_qb + DIL_WIDTH, o_qb + 2 * DIL_WIDTH
    o_q1 = o_qb + 3 * DIL_WIDTH
    o_vc = o_q1 + 4 * DIFF_QK_WIDTH
    o_gate = o_vc + DIFF_V_WIDTH
    o_qa_new = ROTARY_COLS + DIL_WIDTH
    scale = _column_scale(QKV_COLS, [(0, DIL_WIDTH),
                                     (2 * DIL_WIDTH, 2 * DIL_WIDTH + 2 * DIFF_QK_WIDTH),
                                     (o_qa_new, o_qa_new + SB_WIDTH)])

    for l in range(DEPTH):
        wl = w_in[l]
        w_qkv = jnp.concatenate([wl[:, o_qb:o_vb], wl[:, o_q1:o_vc], wl[:, o_vb:o_q1],
                                 wl[:, o_qa:o_qb], wl[:, o_vc:o_gate]], axis=1).astype(BF16)
        w_gate = wl[:, o_gate:].astype(BF16)

        x = _ffn(x, vec(ffn1_norm[l]), ffn1_w_in[l].astype(BF16), ffn1_w_out[l].astype(BF16),
                 vec(final_norm), False)

        gain = vec(mix_norm[l])
        p_dil, p_rope, p_plain = _proj(x, gain, w_qkv, scale, tables)

        y_a = _stick_breaking(p_plain, batch, seq)

        dil_o, dil_lse = [], []
        for g, (window, dil) in enumerate(DIL_PATTERNS):
            assert window // dil == DIL_SPAN
            o, lse = _dilated_group(p_dil, g, batch, seq, dil)
            dil_o.append(o)
            dil_lse.append(lse)

        lam_init = 0.8 - 0.6 * math.exp(-0.3 * l)
        y_c = _diff_attention(p_rope, p_plain,
                              [vec(t[l]) for t in (lam_q1, lam_k1, lam_q2, lam_k2)],
                              vec(diff_subln[l]), batch, seq, lam_init)

        x = _mix_out(x, gain, w_gate, vec(b_gate[l]), y_a, y_c, dil_o, dil_lse,
                     w_up_a[l].astype(BF16), w_up_b[l].astype(BF16), w_up_c[l].astype(BF16),
                     w_out[l].astype(BF16))

        x = _ffn(x, vec(ffn2_norm[l]), ffn2_w_in[l].astype(BF16), ffn2_w_out[l].astype(BF16),
                 vec(final_norm), l == DEPTH - 1)
    return x.reshape(batch, seq, d)
```

```python
import functools
import math

import jax
import jax.numpy as jnp
from jax import lax
from jax.experimental import pallas as pl
from jax.experimental.pallas import tpu as pltpu

F32 = jnp.float32
BF16 = jnp.bfloat16

D_MODEL = 1024
DEPTH = 2
HEAD_DIM = 64
ROT_DIM = HEAD_DIM // 4
ROT_HALF = ROT_DIM // 2
ROPE_THETA = 500000.0
NORM_EPS = 1e-6
SUBLN_EPS = 1e-5
QK_SCALE = HEAD_DIM ** -0.5

SB_HEADS = 8
SB_WIDTH = SB_HEADS * HEAD_DIM
DIL_PATTERNS = ((128, 1), (512, 4), (2048, 16))
DIL_SPAN = 128
DIL_GROUP_WIDTH = 4 * HEAD_DIM
DIL_WIDTH = 3 * DIL_GROUP_WIDTH
DIFF_HEADS = 4
DIFF_QK_WIDTH = DIFF_HEADS * HEAD_DIM
DIFF_V_DIM = 2 * HEAD_DIM
DIFF_V_WIDTH = DIFF_HEADS * DIFF_V_DIM
GATE_WIDTH = 3 * D_MODEL
D_FF = 2816

SB_NEGLIGIBLE = 104.0
LANES = 128
VMEM_LIMIT = 56 * 1024 * 1024

ROTARY_COLS = 2 * DIL_WIDTH + 4 * DIFF_QK_WIDTH
QKV_COLS = ROTARY_COLS + DIL_WIDTH + 3 * SB_WIDTH + DIFF_V_WIDTH
DIL_COLS = 3 * DIL_WIDTH
D_QB, D_KB, D_VB = 0, DIL_WIDTH, 2 * DIL_WIDTH
ROPE_COLS = 4 * DIFF_QK_WIDTH
R_Q1, R_Q2, R_K1, R_K2 = (t * DIFF_QK_WIDTH for t in range(4))
PLAIN_COLS = 3 * SB_WIDTH + DIFF_V_WIDTH
P_QA, P_KA, P_VA, P_VC = 0, SB_WIDTH, 2 * SB_WIDTH, 3 * SB_WIDTH


def _params(*sem):
    return pltpu.CompilerParams(dimension_semantics=sem, vmem_limit_bytes=VMEM_LIMIT)


def _rmsnorm(x, g, eps):
    ms = jnp.mean(x * x, axis=-1, keepdims=True)
    return x * lax.rsqrt(ms + eps) * g


def _rope_table_kernel(pos_ref, invf_ref, cos_ref, sin_ref):
    ang = pos_ref[...].astype(F32) * invf_ref[...]
    lane = lax.broadcasted_iota(jnp.int32, ang.shape, 1) % HEAD_DIM
    c = jnp.cos(ang)
    s = jnp.sin(ang)
    cos_ref[...] = jnp.where(lane < ROT_DIM, c, 1.0)
    sin_ref[...] = jnp.where(lane < ROT_HALF, -s, jnp.where(lane < ROT_DIM, s, 0.0))


def _rope_tables(positions):
    rows = positions.size
    tm = 1024
    inv_freq = ROPE_THETA ** (-jnp.arange(0, ROT_DIM, 2, dtype=F32) / ROT_DIM)
    lane = jnp.arange(LANES) % HEAD_DIM
    invf = jnp.where(lane < ROT_DIM, inv_freq[lane % ROT_HALF], 0.0).reshape(1, LANES)
    return pl.pallas_call(
        _rope_table_kernel,
        grid=(rows // tm,),
        in_specs=[pl.BlockSpec((tm, 1), lambda i: (i, 0)),
                  pl.BlockSpec((1, LANES), lambda i: (0, 0))],
        out_specs=[pl.BlockSpec((tm, LANES), lambda i: (i, 0))] * 2,
        out_shape=[jax.ShapeDtypeStruct((rows, LANES), F32)] * 2,
        compiler_params=_params("parallel"),
        name="rope_tables",
    )(positions.reshape(rows, 1), invf)


def _ffn_kernel(x_ref, g_ref, wg_ref, wu_ref, wo_ref, fg_ref, o_ref, *, final):
    x = x_ref[...]
    n = _rmsnorm(x, g_ref[...], NORM_EPS).astype(BF16)
    gate = jnp.dot(n, wg_ref[...], preferred_element_type=F32)
    up = jnp.dot(n, wu_ref[...], preferred_element_type=F32)
    h = (gate * jax.nn.sigmoid(gate) * up).astype(BF16)
    y = x + 0.5 * jnp.dot(h, wo_ref[...], preferred_element_type=F32)
    if final:
        y = _rmsnorm(y, fg_ref[...], NORM_EPS)
    o_ref[...] = y


def _resident(shape, index_map):
    return pl.BlockSpec(shape, index_map, pipeline_mode=pl.Buffered(1))


def _ffn(x, gain, w_in, w_out, final_gain, final):
    rows = x.shape[0]
    tm = 512
    return pl.pallas_call(
        functools.partial(_ffn_kernel, final=final),
        grid=(rows // tm,),
        in_specs=[pl.BlockSpec((tm, D_MODEL), lambda i: (i, 0)),
                  _resident((1, D_MODEL), lambda i: (0, 0)),
                  _resident((D_MODEL, D_FF), lambda i: (0, 0)),
                  _resident((D_MODEL, D_FF), lambda i: (0, 1)),
                  _resident((D_FF, D_MODEL), lambda i: (0, 0)),
                  _resident((1, D_MODEL), lambda i: (0, 0))],
        out_specs=pl.BlockSpec((tm, D_MODEL), lambda i: (i, 0)),
        out_shape=jax.ShapeDtypeStruct((rows, D_MODEL), F32),
        compiler_params=_params("parallel"),
        name="ffn_final" if final else "ffn",
    )(x, gain, w_in, w_in, w_out, final_gain)


def _proj_kernel(x_ref, g_ref, w_ref, sc_ref, cos_ref, sin_ref, dil_ref, rope_ref, plain_ref):
    n = _rmsnorm(x_ref[...], g_ref[...], NORM_EPS).astype(BF16)
    rot = jnp.dot(n, w_ref[:, :ROTARY_COLS], preferred_element_type=F32) * sc_ref[:, :ROTARY_COLS]
    rest = jnp.dot(n, w_ref[:, ROTARY_COLS:], preferred_element_type=F32) * sc_ref[:, ROTARY_COLS:]
    cos = cos_ref[...]
    sin = sin_ref[...]
    first_half = lax.broadcasted_iota(jnp.int32, cos.shape, 1) % HEAD_DIM < ROT_HALF
    for c in range(ROTARY_COLS // LANES):
        t = rot[:, c * LANES:(c + 1) * LANES]
        partner = jnp.where(first_half,
                            pltpu.roll(t, LANES - ROT_HALF, axis=1),
                            pltpu.roll(t, ROT_HALF, axis=1))
        roped = t * cos + partner * sin
        lo = c * LANES
        if lo < D_VB:
            dil_ref[:, lo:lo + LANES] = roped
        else:
            rope_ref[:, lo - D_VB:lo - D_VB + LANES] = roped.astype(rope_ref.dtype)
    dil_ref[:, D_VB:] = rest[:, :DIL_WIDTH]
    plain_ref[...] = rest[:, DIL_WIDTH:].astype(plain_ref.dtype)


def _proj(x, gain, w, scale, tables):
    rows = x.shape[0]
    tm = 512
    row = lambda width: pl.BlockSpec((tm, width), lambda i: (i, 0))
    const = lambda a: _resident(a.shape, lambda i: (0, 0))
    return pl.pallas_call(
        _proj_kernel,
        grid=(rows // tm,),
        in_specs=[row(D_MODEL), const(gain), const(w), const(scale), row(LANES), row(LANES)],
        out_specs=[row(DIL_COLS), row(ROPE_COLS), row(PLAIN_COLS)],
        out_shape=[jax.ShapeDtypeStruct((rows, DIL_COLS), F32),
                   jax.ShapeDtypeStruct((rows, ROPE_COLS), BF16),
                   jax.ShapeDtypeStruct((rows, PLAIN_COLS), BF16)],
        compiler_params=_params("parallel"),
        name="proj",
    )(x, gain, w, scale, *tables)


def _nt_dot(a, b):
    return lax.dot_general(a, b, (((1,), (1,)), ((), ())), preferred_element_type=F32)


def _tn_dot(a, b):
    return lax.dot_general(a, b, (((0,), (0,)), ((), ())), preferred_element_type=F32)


def _sb_kernel(q_ref, k_ref, v_ref, o_ref, *, tq, npairs):
    i = pl.program_id(2)
    lane = lax.broadcasted_iota(jnp.int32, (1, LANES), 1)
    halves = (lane < HEAD_DIM, lane >= HEAD_DIM)
    chains = [(pp, h) for pp in range(npairs) for h in range(2)]
    pair_cols = [slice(pp * LANES, (pp + 1) * LANES) for pp in range(npairs)]
    q_pairs = [q_ref[:, cols] for cols in pair_cols]
    qms = [jnp.where(halves[h], q_pairs[pp], jnp.zeros_like(q_pairs[pp])) for pp, h in chains]
    key = lax.broadcasted_iota(jnp.int32, (tq, tq), 0)
    qry = lax.broadcasted_iota(jnp.int32, (tq, tq), 1)
    later_key = (qry > key).astype(BF16)
    strict = key < qry

    def block(j, state, masked):
        start = pl.multiple_of(j * tq, tq)
        kbs = [k_ref[pl.ds(start, tq), cols] for cols in pair_cols]
        vbs = [v_ref[pl.ds(start, tq), cols] for cols in pair_cols]
        z = [_nt_dot(kbs[pp], qm) for (pp, h), qm in zip(chains, qms)]
        log_beta, split, sums = [], [], []
        for x in z:
            u = jnp.maximum(x, 0.0) + jnp.log(1.0 + jnp.exp(-jnp.abs(x)))
            if masked:
                u = jnp.where(strict, u, 0.0)
            hi = u.astype(BF16)
            split += [hi, (u - hi.astype(F32)).astype(BF16)]
            log_beta.append(x - u)
            sums.append(jnp.sum(u, axis=0, keepdims=True))
        parts = jnp.dot(later_key, jnp.concatenate(split, axis=1), preferred_element_type=F32)
        w = []
        for c, (lb, st) in enumerate(zip(log_beta, state)):
            after = parts[:, 2 * c * tq:(2 * c + 1) * tq] + parts[:, (2 * c + 1) * tq:(2 * c + 2) * tq] + st[1]
            wc = jnp.exp(lb - after)
            if masked:
                wc = jnp.where(strict, wc, 0.0)
            w.append(wc.astype(BF16))
        pv = [_tn_dot(vbs[pp], jnp.concatenate(w[2 * pp:2 * pp + 2], axis=1)) for pp in range(npairs)]
        return tuple((st[0] + pv[pp][:, h * tq:(h + 1) * tq], st[1] + y)
                     for (pp, h), st, y in zip(chains, state, sums))

    def smallest_carry(state):
        low = state[0][1]
        for st in state[1:]:
            low = jnp.minimum(low, st[1])
        return jnp.min(low)

    def more(c):
        j, _, low = c
        return (j >= 0) & (low < SB_NEGLIGIBLE)

    def step(c):
        j, st, _ = c
        st = block(j, st, False)
        return j - 1, st, smallest_carry(st)

    zero = (jnp.zeros((LANES, tq), F32), jnp.zeros((1, tq), F32))
    state = block(i, (zero,) * len(chains), True)
    _, state, _ = lax.while_loop(more, step, (i - 1, state, smallest_carry(state)))
    feat = lax.broadcasted_iota(jnp.int32, (LANES, 1), 0)
    for pp, cols in enumerate(pair_cols):
        o_ref[:, cols] = jnp.where(feat < HEAD_DIM, state[2 * pp][0], state[2 * pp + 1][0]).T.astype(o_ref.dtype)


def _stick_breaking(p_plain, batch, seq):
    tq = 256
    npairs = 4
    nq = seq // tq
    width = npairs * LANES
    groups = SB_WIDTH // width
    kcol, vcol = P_KA // width, P_VA // width
    return pl.pallas_call(
        functools.partial(_sb_kernel, tq=tq, npairs=npairs),
        grid=(batch, groups, nq),
        in_specs=[pl.BlockSpec((tq, width), lambda b, p, i: (b * nq + i, p)),
                  pl.BlockSpec((seq, width), lambda b, p, i: (b, kcol + p)),
                  pl.BlockSpec((seq, width), lambda b, p, i: (b, vcol + p))],
        out_specs=pl.BlockSpec((tq, width), lambda b, p, i: (b * nq + i, p)),
        out_shape=jax.ShapeDtypeStruct((batch * seq, SB_WIDTH), BF16),
        compiler_params=_params("parallel", "parallel", "arbitrary"),
        name="stick_breaking",
    )(p_plain, p_plain, p_plain)


def _rows_of(start, size, stride):
    return pl.ds(start, size) if stride == 1 else pl.ds(start, size, stride=stride)


def _dil_kernel(*refs, dil, nsub, has_prev):
    if has_prev:
        q_ref, kc_ref, vc_ref, kp_ref, vp_ref, o_ref, lse_ref = refs
    else:
        q_ref, kc_ref, vc_ref, o_ref, lse_ref = refs
        kp_ref = vp_ref = None
    n = pl.program_id(2)
    span = DIL_SPAN
    gw = LANES
    nkeys = 2 * span if has_prev else span
    qi = lax.broadcasted_iota(jnp.int32, (span, nkeys), 0) + (nkeys - span)
    kj = lax.broadcasted_iota(jnp.int32, (span, nkeys), 1)
    dist = qi - kj
    band = (dist >= 0) & (dist <= span)
    if has_prev:
        first_key = jnp.where(n > 0, 0, span)
        band_first = band & (kj >= first_key)
    lane = lax.broadcasted_iota(jnp.int32, (1, gw), 1)
    heads = [(lane >= h * HEAD_DIM) & (lane < (h + 1) * HEAD_DIM) for h in range(gw // HEAD_DIM)]

    def tile_rows(r, c):
        return _rows_of(r * span * dil + c, span, dil)

    def keys_of(cur_ref, prev_ref, r, c):
        own = cur_ref[tile_rows(r, c), :]
        if not has_prev:
            return own.astype(BF16)
        before = cur_ref[tile_rows(r - 1, c), :] if r > 0 else prev_ref[_rows_of(c, span, dil), :]
        return jnp.concatenate([before, own], axis=0).astype(BF16)

    all_tiles = [(r, c) for r in range(nsub) for c in range(dil)]
    per_pass = 8
    for t0 in range(0, len(all_tiles), per_pass):
        tiles = all_tiles[t0:t0 + per_pass]
        chains = [(t, h) for t in range(len(tiles)) for h in range(len(heads))]
        qs = [q_ref[tile_rows(r, c), :].astype(BF16) for r, c in tiles]
        ks = [keys_of(kc_ref, kp_ref, r, c) for r, c in tiles]
        vs = [keys_of(vc_ref, vp_ref, r, c) for r, c in tiles]
        s = [_nt_dot(jnp.where(heads[h], qs[t], jnp.zeros_like(qs[t])), ks[t]) for t, h in chains]
        s = [jnp.where(band_first if has_prev and tiles[t][0] == 0 else band, x, -jnp.inf)
             for x, (t, h) in zip(s, chains)]
        m = [jnp.max(x, axis=-1, keepdims=True) for x in s]
        p = [jnp.exp(x - mx) for x, mx in zip(s, m)]
        l = [jnp.sum(x, axis=-1, keepdims=True) for x in p]
        o = [jnp.dot(x.astype(BF16), vs[t], preferred_element_type=F32) for x, (t, h) in zip(p, chains)]
        o = [x * (1.0 / lx) for x, lx in zip(o, l)]
        lse = [mx + jnp.log(lx) for mx, lx in zip(m, l)]
        for t, (r, c) in enumerate(tiles):
            out = jnp.zeros((span, gw), F32)
            lse_out = jnp.zeros((span, gw), F32)
            for h in range(len(heads)):
                out = jnp.where(heads[h], o[t * len(heads) + h], out)
                lse_out = jnp.where(heads[h], lse[t * len(heads) + h], lse_out)
            o_ref[tile_rows(r, c), :] = out
            lse_ref[tile_rows(r, c), :] = lse_out


def _dilated_group(p_dil, group, batch, seq, dil):
    span = DIL_SPAN
    pairs = DIL_GROUP_WIDTH // LANES
    nsub = max(1, 8 // dil)
    tb = nsub * span * dil
    pb = span * dil
    has_prev = tb < seq
    nb = seq // tb
    qcol, kcol, vcol = (c // LANES + group * pairs for c in (D_QB, D_KB, D_VB))
    cur = lambda col: pl.BlockSpec((tb, LANES), lambda b, u, n: (b * nb + n, col + u))
    prev = lambda col: pl.BlockSpec(
        (pb, LANES), lambda b, u, n: (b * (seq // pb) + jnp.maximum(n * (tb // pb) - 1, 0), col + u))
    in_specs = [cur(qcol), cur(kcol), cur(vcol)] + ([prev(kcol), prev(vcol)] if has_prev else [])
    return pl.pallas_call(
        functools.partial(_dil_kernel, dil=dil, nsub=nsub, has_prev=has_prev),
        grid=(batch, pairs, nb),
        in_specs=in_specs,
        out_specs=[cur(0), cur(0)],
        out_shape=[jax.ShapeDtypeStruct((batch * seq, DIL_GROUP_WIDTH), F32)] * 2,
        compiler_params=_params("parallel", "parallel", "arbitrary"),
        name=f"dilated_d{dil}",
    )(*([p_dil] * len(in_specs)))


def _diff_kernel(q1_ref, q2_ref, k1_ref, k2_ref, v_ref, lq1_ref, lk1_ref, lq2_ref, lk2_ref, g_ref,
                 o_ref, *, tq, npairs, lam_init):
    i = pl.program_id(2)
    lane = lax.broadcasted_iota(jnp.int32, (1, LANES), 1)
    halves = (lane < HEAD_DIM, lane >= HEAD_DIM)
    heads = [(pp, h) for pp in range(npairs) for h in range(2)]
    chains = [(hd, t) for hd in range(len(heads)) for t in range(2)]
    q_refs = (q1_ref, q2_ref)
    k_refs = (k1_ref, k2_ref)
    pair_cols = [slice(pp * LANES, (pp + 1) * LANES) for pp in range(npairs)]
    head_cols = [slice(hd * DIFF_V_DIM, (hd + 1) * DIFF_V_DIM) for hd in range(len(heads))]
    q_pairs = [[r[:, cols] for cols in pair_cols] for r in q_refs]
    qs = [jnp.where(halves[heads[hd][1]], q_pairs[t][heads[hd][0]], jnp.zeros_like(q_pairs[t][heads[hd][0]]))
          for hd, t in chains]
    key = lax.broadcasted_iota(jnp.int32, (tq, tq), 0)
    qry = lax.broadcasted_iota(jnp.int32, (tq, tq), 1)
    causal = key <= qry

    def block(j, state, masked):
        start = pl.multiple_of(j * tq, tq)
        kbs = [[r[pl.ds(start, tq), cols] for cols in pair_cols] for r in k_refs]
        vbs = [v_ref[pl.ds(start, tq), cols] for cols in head_cols]
        s = [_nt_dot(kbs[t][heads[hd][0]], q) for (hd, t), q in zip(chains, qs)]
        if masked:
            s = [jnp.where(causal, x, -jnp.inf) for x in s]
        m_new = [jnp.maximum(st[0], jnp.max(x, axis=0, keepdims=True)) for st, x in zip(state, s)]
        pb, psum = [], []
        for x, mx in zip(s, m_new):
            p = jnp.exp(x - mx)
            pb.append(p.astype(BF16))
            psum.append(jnp.sum(p, axis=0, keepdims=True))
        pv = [_tn_dot(vbs[hd], jnp.concatenate(pb[2 * hd:2 * hd + 2], axis=1)) for hd in range(len(heads))]
        pv = [pv[hd][:, t * tq:(t + 1) * tq] for hd, t in chains]
        alpha = [jnp.exp(st[0] - mx) for st, mx in zip(state, m_new)]
        l_new = [a * st[1] + y for a, st, y in zip(alpha, state, psum)]
        acc = [a * st[2] + x for a, st, x in zip(alpha, state, pv)]
        return tuple(zip(m_new, l_new, acc))

    init = (jnp.full((1, tq), -jnp.inf, F32), jnp.zeros((1, tq), F32), jnp.zeros((DIFF_V_DIM, tq), F32))
    state = block(i, (init,) * len(chains), True)
    state = lax.fori_loop(0, i, lambda jj, st: block(i - 1 - jj, st, False), state)

    lam = (jnp.exp(jnp.sum(lq1_ref[...] * lk1_ref[...], axis=-1, keepdims=True))
           - jnp.exp(jnp.sum(lq2_ref[...] * lk2_ref[...], axis=-1, keepdims=True)) + lam_init)
    for hd, cols in enumerate(head_cols):
        (_, l1, a1), (_, l2, a2) = state[2 * hd], state[2 * hd + 1]
        o = (a1 * (1.0 / l1) - lam * (a2 * (1.0 / l2))).T
        o_ref[:, cols] = (_rmsnorm(o, g_ref[...], SUBLN_EPS) * (1.0 - lam_init)).astype(o_ref.dtype)


def _diff_attention(p_rope, p_plain, lam_vecs, subln, batch, seq, lam_init):
    tq = 256
    npairs = 2
    nq = seq // tq
    qk_w = npairs * LANES
    v_w = 2 * npairs * DIFF_V_DIM
    groups = DIFF_QK_WIDTH // qk_w
    q1c, q2c, k1c, k2c = (c // qk_w for c in (R_Q1, R_Q2, R_K1, R_K2))
    vcol = P_VC // v_w
    qspec = lambda c0: pl.BlockSpec((tq, qk_w), lambda b, p, i: (b * nq + i, c0 + p))
    kspec = lambda c0: pl.BlockSpec((seq, qk_w), lambda b, p, i: (b, c0 + p))
    vec = lambda w: pl.BlockSpec((1, w), lambda b, p, i: (0, 0))
    return pl.pallas_call(
        functools.partial(_diff_kernel, tq=tq, npairs=npairs, lam_init=lam_init),
        grid=(batch, groups, nq),
        in_specs=[qspec(q1c), qspec(q2c), kspec(k1c), kspec(k2c),
                  pl.BlockSpec((seq, v_w), lambda b, p, i: (b, vcol + p)),
                  vec(HEAD_DIM), vec(HEAD_DIM), vec(HEAD_DIM), vec(HEAD_DIM), vec(DIFF_V_DIM)],
        out_specs=pl.BlockSpec((tq, v_w), lambda b, p, i: (b * nq + i, p)),
        out_shape=jax.ShapeDtypeStruct((batch * seq, DIFF_V_WIDTH), BF16),
        compiler_params=_params("parallel", "parallel", "arbitrary"),
        name="diff_attention",
    )(p_rope, p_rope, p_rope, p_rope, p_plain, *lam_vecs, subln)


def _mix_kernel(x_ref, g_ref, wg_ref, bg_ref, ya_ref, yc_ref,
                o1_ref, o2_ref, o3_ref, s1_ref, s2_ref, s3_ref,
                wa_ref, wb_ref, wc_ref, wo_ref, out_ref):
    x = x_ref[...]
    hn = _rmsnorm(x, g_ref[...], NORM_EPS).astype(BF16)
    gates = jax.nn.sigmoid(jnp.dot(hn, wg_ref[...], preferred_element_type=F32) + bg_ref[...])
    lses = [s1_ref[...], s2_ref[...], s3_ref[...]]
    m = jnp.maximum(jnp.maximum(lses[0], lses[1]), lses[2])
    es = [jnp.exp(s - m) for s in lses]
    den = es[0] + es[1] + es[2]
    yb = (es[0] / den) * o1_ref[...] + (es[1] / den) * o2_ref[...] + (es[2] / den) * o3_ref[...]
    up_a = jnp.dot(ya_ref[...], wa_ref[...], preferred_element_type=F32)
    up_b = jnp.dot(yb.astype(BF16), wb_ref[...], preferred_element_type=F32)
    up_c = jnp.dot(yc_ref[...], wc_ref[...], preferred_element_type=F32)
    d = D_MODEL
    merged = gates[:, :d] * up_a + gates[:, d:2 * d] * up_b + gates[:, 2 * d:] * up_c
    out_ref[...] = x + jnp.dot(merged.astype(BF16), wo_ref[...], preferred_element_type=F32)


def _mix_out(x, gain, w_gate, b_gate, y_a, y_c, dil_o, dil_lse, w_up_a, w_up_b, w_up_c, w_out):
    rows = x.shape[0]
    tm = 512
    row = lambda w: pl.BlockSpec((tm, w), lambda i: (i, 0))
    full = lambda a: _resident(a.shape, lambda i: (0, 0))
    gw = DIL_GROUP_WIDTH
    return pl.pallas_call(
        _mix_kernel,
        grid=(rows // tm,),
        in_specs=[row(D_MODEL), full(gain), full(w_gate), full(b_gate), row(SB_WIDTH), row(DIFF_V_WIDTH),
                  row(gw), row(gw), row(gw), row(gw), row(gw), row(gw),
                  full(w_up_a), full(w_up_b), full(w_up_c), full(w_out)],
        out_specs=row(D_MODEL),
        out_shape=jax.ShapeDtypeStruct((rows, D_MODEL), F32),
        compiler_params=_params("parallel"),
        name="mix_out",
    )(x, gain, w_gate, b_gate, y_a, y_c, *dil_o, *dil_lse, w_up_a, w_up_b, w_up_c, w_out)


def _column_scale(width, scaled_ranges):
    col = jnp.arange(width)
    s = jnp.ones((width,), F32)
    for lo, hi in scaled_ranges:
        s = jnp.where((col >= lo) & (col < hi), QK_SCALE, s)
    return s.reshape(1, width)


def kernel(x, positions, ffn1_norm, ffn1_w_in, ffn1_w_out, mix_norm, w_in, b_gate, lam_q1, lam_k1, lam_q2, lam_k2, diff_subln, w_up_a, w_up_b, w_up_c, w_out, ffn2_norm, ffn2_w_in, ffn2_w_out, final_norm):
    batch, seq, d = x.shape
    rows = batch * seq
    x = x.reshape(rows, d)
    tables = _rope_tables(positions)
    vec = lambda t: t.reshape(1, -1)

    o_qa, o_ka, o_va = 0, SB_WIDTH, 2 * SB_WIDTH
    o_qb = 3 * SB_WIDTH
    o_kb, o_vb = o_qb + DIL_WIDTH, o_qb + 2 * DIL_WIDTH
    o_q1 = o_qb + 3 * DIL_WIDTH
    o_vc = o_q1 + 4 * DIFF_QK_WIDTH
    o_gate = o_vc + DIFF_V_WIDTH
    o_qa_new = ROTARY_COLS + DIL_WIDTH
    scale = _column_scale(QKV_COLS, [(0, DIL_WIDTH),
                                     (2 * DIL_WIDTH, 2 * DIL_WIDTH + 2 * DIFF_QK_WIDTH),
                                     (o_qa_new, o_qa_new + SB_WIDTH)])

    for l in range(DEPTH):
        wl = w_in[l]
        w_qkv = jnp.concatenate([wl[:, o_qb:o_vb], wl[:, o_q1:o_vc], wl[:, o_vb:o_q1],
                                 wl[:, o_qa:o_qb], wl[:, o_vc:o_gate]], axis=1).astype(BF16)
        w_gate = wl[:, o_gate:].astype(BF16)

        x = _ffn(x, vec(ffn1_norm[l]), ffn1_w_in[l].astype(BF16), ffn1_w_out[l].astype(BF16),
                 vec(final_norm), False)

        gain = vec(mix_norm[l])
        p_dil, p_rope, p_plain = _proj(x, gain, w_qkv, scale, tables)

        y_a = _stick_breaking(p_plain, batch, seq)

        dil_o, dil_lse = [], []
        for g, (window, dil) in enumerate(DIL_PATTERNS):
            assert window // dil == DIL_SPAN
            o, lse = _dilated_group(p_dil, g, batch, seq, dil)
            dil_o.append(o)
            dil_lse.append(lse)

        lam_init = 0.8 - 0.6 * math.exp(-0.3 * l)
        y_c = _diff_attention(p_rope, p_plain,
                              [vec(t[l]) for t in (lam_q1, lam_k1, lam_q2, lam_k2)],
                              vec(diff_subln[l]), batch, seq, lam_init)

        x = _mix_out(x, gain, w_gate, vec(b_gate[l]), y_a, y_c, dil_o, dil_lse,
                     w_up_a[l].astype(BF16), w_up_b[l].astype(BF16), w_up_c[l].astype(BF16),
                     w_out[l].astype(BF16))

        x = _ffn(x, vec(ffn2_norm[l]), ffn2_w_in[l].astype(BF16), ffn2_w_out[l].astype(BF16),
                 vec(final_norm), l == DEPTH - 1)
    return x.reshape(batch, seq, d)
```

```python
import functools
import math

import jax
import jax.numpy as jnp
from jax import lax
from jax.experimental import pallas as pl
from jax.experimental.pallas import tpu as pltpu

F32 = jnp.float32
BF16 = jnp.bfloat16

D_MODEL = 1024
DEPTH = 2
HEAD_DIM = 64
ROT_DIM = HEAD_DIM // 4
ROT_HALF = ROT_DIM // 2
ROPE_THETA = 500000.0
NORM_EPS = 1e-6
SUBLN_EPS = 1e-5
QK_SCALE = HEAD_DIM ** -0.5

SB_HEADS = 8
SB_WIDTH = SB_HEADS * HEAD_DIM
DIL_PATTERNS = ((128, 1), (512, 4), (2048, 16))
DIL_SPAN = 128
DIL_GROUP_WIDTH = 4 * HEAD_DIM
DIL_WIDTH = 3 * DIL_GROUP_WIDTH
DIFF_HEADS = 4
DIFF_QK_WIDTH = DIFF_HEADS * HEAD_DIM
DIFF_V_DIM = 2 * HEAD_DIM
DIFF_V_WIDTH = DIFF_HEADS * DIFF_V_DIM
GATE_WIDTH = 3 * D_MODEL
D_FF = 2816

SB_NEGLIGIBLE = 104.0
LANES = 128
VMEM_LIMIT = 56 * 1024 * 1024

ROTARY_COLS = 2 * DIL_WIDTH + 4 * DIFF_QK_WIDTH
QKV_COLS = ROTARY_COLS + DIL_WIDTH + 3 * SB_WIDTH + DIFF_V_WIDTH
DIL_COLS = 3 * DIL_WIDTH
D_QB, D_KB, D_VB = 0, DIL_WIDTH, 2 * DIL_WIDTH
ROPE_COLS = 4 * DIFF_QK_WIDTH
R_Q1, R_Q2, R_K1, R_K2 = (t * DIFF_QK_WIDTH for t in range(4))
PLAIN_COLS = 3 * SB_WIDTH + DIFF_V_WIDTH
P_QA, P_KA, P_VA, P_VC = 0, SB_WIDTH, 2 * SB_WIDTH, 3 * SB_WIDTH


def _params(*sem):
    return pltpu.CompilerParams(dimension_semantics=sem, vmem_limit_bytes=VMEM_LIMIT)


def _rmsnorm(x, g, eps):
    ms = jnp.mean(x * x, axis=-1, keepdims=True)
    return x * lax.rsqrt(ms + eps) * g


def _rope_table_kernel(pos_ref, invf_ref, cos_ref, sin_ref):
    ang = pos_ref[...].astype(F32) * invf_ref[...]
    lane = lax.broadcasted_iota(jnp.int32, ang.shape, 1) % HEAD_DIM
    c = jnp.cos(ang)
    s = jnp.sin(ang)
    cos_ref[...] = jnp.where(lane < ROT_DIM, c, 1.0)
    sin_ref[...] = jnp.where(lane < ROT_HALF, -s, jnp.where(lane < ROT_DIM, s, 0.0))


def _rope_tables(positions):
    rows = positions.size
    tm = 1024
    inv_freq = ROPE_THETA ** (-jnp.arange(0, ROT_DIM, 2, dtype=F32) / ROT_DIM)
    lane = jnp.arange(LANES) % HEAD_DIM
    invf = jnp.where(lane < ROT_DIM, inv_freq[lane % ROT_HALF], 0.0).reshape(1, LANES)
    return pl.pallas_call(
        _rope_table_kernel,
        grid=(rows // tm,),
        in_specs=[pl.BlockSpec((tm, 1), lambda i: (i, 0)),
                  pl.BlockSpec((1, LANES), lambda i: (0, 0))],
        out_specs=[pl.BlockSpec((tm, LANES), lambda i: (i, 0))] * 2,
        out_shape=[jax.ShapeDtypeStruct((rows, LANES), F32)] * 2,
        compiler_params=_params("parallel"),
        name="rope_tables",
    )(positions.reshape(rows, 1), invf)


def _ffn_kernel(x_ref, g_ref, wg_ref, wu_ref, wo_ref, fg_ref, o_ref, *, final):
    x = x_ref[...]
    n = _rmsnorm(x, g_ref[...], NORM_EPS).astype(BF16)
    gate = jnp.dot(n, wg_ref[...], preferred_element_type=F32)
    up = jnp.dot(n, wu_ref[...], preferred_element_type=F32)
    h = (gate * jax.nn.sigmoid(gate) * up).astype(BF16)
    y = x + 0.5 * jnp.dot(h, wo_ref[...], preferred_element_type=F32)
    if final:
        y = _rmsnorm(y, fg_ref[...], NORM_EPS)
    o_ref[...] = y


def _resident(shape, index_map):
    return pl.BlockSpec(shape, index_map, pipeline_mode=pl.Buffered(1))


def _ffn(x, gain, w_in, w_out, final_gain, final):
    rows = x.shape[0]
    tm = 512
    return pl.pallas_call(
        functools.partial(_ffn_kernel, final=final),
        grid=(rows // tm,),
        in_specs=[pl.BlockSpec((tm, D_MODEL), lambda i: (i, 0)),
                  _resident((1, D_MODEL), lambda i: (0, 0)),
                  _resident((D_MODEL, D_FF), lambda i: (0, 0)),
                  _resident((D_MODEL, D_FF), lambda i: (0, 1)),
                  _resident((D_FF, D_MODEL), lambda i: (0, 0)),
                  _resident((1, D_MODEL), lambda i: (0, 0))],
        out_specs=pl.BlockSpec((tm, D_MODEL), lambda i: (i, 0)),
        out_shape=jax.ShapeDtypeStruct((rows, D_MODEL), F32),
        compiler_params=_params("parallel"),
        name="ffn_final" if final else "ffn",
    )(x, gain, w_in, w_in, w_out, final_gain)


def _proj_kernel(x_ref, g_ref, w_ref, sc_ref, cos_ref, sin_ref, dil_ref, rope_ref, plain_ref):
    n = _rmsnorm(x_ref[...], g_ref[...], NORM_EPS).astype(BF16)
    rot = jnp.dot(n, w_ref[:, :ROTARY_COLS], preferred_element_type=F32) * sc_ref[:, :ROTARY_COLS]
    rest = jnp.dot(n, w_ref[:, ROTARY_COLS:], preferred_element_type=F32) * sc_ref[:, ROTARY_COLS:]
    cos = cos_ref[...]
    sin = sin_ref[...]
    first_half = lax.broadcasted_iota(jnp.int32, cos.shape, 1) % HEAD_DIM < ROT_HALF
    for c in range(ROTARY_COLS // LANES):
        t = rot[:, c * LANES:(c + 1) * LANES]
        partner = jnp.where(first_half,
                            pltpu.roll(t, LANES - ROT_HALF, axis=1),
                            pltpu.roll(t, ROT_HALF, axis=1))
        roped = t * cos + partner * sin
        lo = c * LANES
        if lo < D_VB:
            dil_ref[:, lo:lo + LANES] = roped
        else:
            rope_ref[:, lo - D_VB:lo - D_VB + LANES] = roped.astype(rope_ref.dtype)
    dil_ref[:, D_VB:] = rest[:, :DIL_WIDTH]
    plain_ref[...] = rest[:, DIL_WIDTH:].astype(plain_ref.dtype)


def _proj(x, gain, w, scale, tables):
    rows = x.shape[0]
    tm = 512
    row = lambda width: pl.BlockSpec((tm, width), lambda i: (i, 0))
    const = lambda a: _resident(a.shape, lambda i: (0, 0))
    return pl.pallas_call(
        _proj_kernel,
        grid=(rows // tm,),
        in_specs=[row(D_MODEL), const(gain), const(w), const(scale), row(LANES), row(LANES)],
        out_specs=[row(DIL_COLS), row(ROPE_COLS), row(PLAIN_COLS)],
        out_shape=[jax.ShapeDtypeStruct((rows, DIL_COLS), F32),
                   jax.ShapeDtypeStruct((rows, ROPE_COLS), BF16),
                   jax.ShapeDtypeStruct((rows, PLAIN_COLS), BF16)],
        compiler_params=_params("parallel"),
        name="proj",
    )(x, gain, w, scale, *tables)


def _nt_dot(a, b):
    return lax.dot_general(a, b, (((1,), (1,)), ((), ())), preferred_element_type=F32)


def _tn_dot(a, b):
    return lax.dot_general(a, b, (((0,), (0,)), ((), ())), preferred_element_type=F32)


def _sb_kernel(q_ref, k_ref, v_ref, o_ref, *, tq, npairs):
    i = pl.program_id(2)
    lane = lax.broadcasted_iota(jnp.int32, (1, LANES), 1)
    halves = (lane < HEAD_DIM, lane >= HEAD_DIM)
    chains = [(pp, h) for pp in range(npairs) for h in range(2)]
    pair_cols = [slice(pp * LANES, (pp + 1) * LANES) for pp in range(npairs)]
    q_pairs = [q_ref[:, cols] for cols in pair_cols]
    qms = [jnp.where(halves[h], q_pairs[pp], jnp.zeros_like(q_pairs[pp])) for pp, h in chains]
    key = lax.broadcasted_iota(jnp.int32, (tq, tq), 0)
    qry = lax.broadcasted_iota(jnp.int32, (tq, tq), 1)
    later_key = (qry > key).astype(BF16)
    strict = key < qry

    def block(j, state, masked):
        start = pl.multiple_of(j * tq, tq)
        kbs = [k_ref[pl.ds(start, tq), cols] for cols in pair_cols]
        vbs = [v_ref[pl.ds(start, tq), cols] for cols in pair_cols]
        z = [_nt_dot(kbs[pp], qm) for (pp, h), qm in zip(chains, qms)]
        log_beta, rounded, sums = [], [], []
        for x in z:
            u = jnp.maximum(x, 0.0) + jnp.log(1.0 + jnp.exp(-jnp.abs(x)))
            if masked:
                u = jnp.where(strict, u, 0.0)
            rounded.append(u.astype(BF16))
            log_beta.append(x - u)
            sums.append(jnp.sum(u, axis=0, keepdims=True))
        parts = jnp.dot(later_key, jnp.concatenate(rounded, axis=1), preferred_element_type=F32)
        w = []
        for c, (lb, st) in enumerate(zip(log_beta, state)):
            after = parts[:, c * tq:(c + 1) * tq] + st[1]
            wc = jnp.exp(lb - after)
            if masked:
                wc = jnp.where(strict, wc, 0.0)
            w.append(wc.astype(BF16))
        pv = [_tn_dot(vbs[pp], jnp.concatenate(w[2 * pp:2 * pp + 2], axis=1)) for pp in range(npairs)]
        return tuple((st[0] + pv[pp][:, h * tq:(h + 1) * tq], st[1] + y)
                     for (pp, h), st, y in zip(chains, state, sums))

    def smallest_carry(state):
        low = state[0][1]
        for st in state[1:]:
            low = jnp.minimum(low, st[1])
        return jnp.min(low)

    def more(c):
        j, _, low = c
        return (j >= 0) & (low < SB_NEGLIGIBLE)

    def step(c):
        j, st, _ = c
        st = block(j, st, False)
        return j - 1, st, smallest_carry(st)

    zero = (jnp.zeros((LANES, tq), F32), jnp.zeros((1, tq), F32))
    state = block(i, (zero,) * len(chains), True)
    _, state, _ = lax.while_loop(more, step, (i - 1, state, smallest_carry(state)))
    feat = lax.broadcasted_iota(jnp.int32, (LANES, 1), 0)
    for pp, cols in enumerate(pair_cols):
        o_ref[:, cols] = jnp.where(feat < HEAD_DIM, state[2 * pp][0], state[2 * pp + 1][0]).T.astype(o_ref.dtype)


def _stick_breaking(p_plain, batch, seq):
    tq = 256
    npairs = 4
    nq = seq // tq
    width = npairs * LANES
    groups = SB_WIDTH // width
    kcol, vcol = P_KA // width, P_VA // width
    return pl.pallas_call(
        functools.partial(_sb_kernel, tq=tq, npairs=npairs),
        grid=(batch, groups, nq),
        in_specs=[pl.BlockSpec((tq, width), lambda b, p, i: (b * nq + i, p)),
                  pl.BlockSpec((seq, width), lambda b, p, i: (b, kcol + p)),
                  pl.BlockSpec((seq, width), lambda b, p, i: (b, vcol + p))],
        out_specs=pl.BlockSpec((tq, width), lambda b, p, i: (b * nq + i, p)),
        out_shape=jax.ShapeDtypeStruct((batch * seq, SB_WIDTH), BF16),
        compiler_params=_params("parallel", "parallel", "arbitrary"),
        name="stick_breaking",
    )(p_plain, p_plain, p_plain)


def _rows_of(start, size, stride):
    return pl.ds(start, size) if stride == 1 else pl.ds(start, size, stride=stride)


def _dil_kernel(*refs):
    qkv_refs, (y_ref, o_scr, lse_scr) = refs[:-3], refs[-3:]
    span = DIL_SPAN
    seq = y_ref.shape[0]
    lane = lax.broadcasted_iota(jnp.int32, (1, LANES), 1)
    heads = [(lane >= h * HEAD_DIM) & (lane < (h + 1) * HEAD_DIM) for h in range(LANES // HEAD_DIM)]

    def band(nkeys):
        qi = lax.broadcasted_iota(jnp.int32, (span, nkeys), 0) + (nkeys - span)
        kj = lax.broadcasted_iota(jnp.int32, (span, nkeys), 1)
        return (qi - kj >= 0) & (qi - kj <= span)

    bands = {span: band(span), 2 * span: band(2 * span)}

    for g, (window, dil) in enumerate(DIL_PATTERNS):
        q_ref, k_ref, v_ref = qkv_refs[3 * g:3 * g + 3]

        def tile_rows(r, c):
            return _rows_of(r * span * dil + c, span, dil)

        def keys_of(ref, r, c):
            own = ref[tile_rows(r, c), :]
            if r == 0:
                return own.astype(BF16)
            return jnp.concatenate([ref[tile_rows(r - 1, c), :], own], axis=0).astype(BF16)

        all_tiles = [(r, c) for r in range(seq // (span * dil)) for c in range(dil)]
        per_pass = 8
        for t0 in range(0, len(all_tiles), per_pass):
            tiles = all_tiles[t0:t0 + per_pass]
            chains = [(t, h) for t in range(len(tiles)) for h in range(len(heads))]
            qs = [q_ref[tile_rows(r, c), :].astype(BF16) for r, c in tiles]
            ks = [keys_of(k_ref, r, c) for r, c in tiles]
            vs = [keys_of(v_ref, r, c) for r, c in tiles]
            s = [_nt_dot(jnp.where(heads[h], qs[t], jnp.zeros_like(qs[t])), ks[t]) for t, h in chains]
            s = [jnp.where(bands[x.shape[1]], x, -jnp.inf) for x in s]
            m = [jnp.max(x, axis=-1, keepdims=True) for x in s]
            p = [jnp.exp(x - mx) for x, mx in zip(s, m)]
            l = [jnp.sum(x, axis=-1, keepdims=True) for x in p]
            o = [jnp.dot(x.astype(BF16), vs[t], preferred_element_type=F32) for x, (t, h) in zip(p, chains)]
            o = [x * (1.0 / lx) for x, lx in zip(o, l)]
            lse = [mx + jnp.log(lx) for mx, lx in zip(m, l)]
            for t, (r, c) in enumerate(tiles):
                out = jnp.zeros((span, LANES), F32)
                lse_out = jnp.zeros((span, LANES), F32)
                for h in range(len(heads)):
                    out = jnp.where(heads[h], o[t * len(heads) + h], out)
                    lse_out = jnp.where(heads[h], lse[t * len(heads) + h], lse_out)
                o_scr[g, tile_rows(r, c), :] = out
                lse_scr[g, tile_rows(r, c), :] = lse_out

    chunk = 2 * span
    for r0 in range(0, seq, chunk):
        rows = slice(r0, r0 + chunk)
        lses = [lse_scr[g, rows, :] for g in range(len(DIL_PATTERNS))]
        top = functools.reduce(jnp.maximum, lses)
        es = [jnp.exp(x - top) for x in lses]
        inv = 1.0 / functools.reduce(lambda a, b: a + b, es)
        y = functools.reduce(lambda a, b: a + b, [(e * inv) * o_scr[g, rows, :] for g, e in enumerate(es)])
        y_ref[rows, :] = y.astype(y_ref.dtype)


def _dilated(p_dil, batch, seq):
    pairs = DIL_GROUP_WIDTH // LANES
    groups = len(DIL_PATTERNS)
    assert all(window // dil == DIL_SPAN and seq % (DIL_SPAN * dil) == 0 for window, dil in DIL_PATTERNS)
    spec = lambda col: pl.BlockSpec((seq, LANES), lambda b, u: (b, col + u))
    in_specs = [spec(c // LANES + g * pairs) for g in range(groups) for c in (D_QB, D_KB, D_VB)]
    return pl.pallas_call(
        _dil_kernel,
        grid=(batch, pairs),
        in_specs=in_specs,
        out_specs=spec(0),
        out_shape=jax.ShapeDtypeStruct((batch * seq, DIL_GROUP_WIDTH), BF16),
        scratch_shapes=[pltpu.VMEM((groups, seq, LANES), F32), pltpu.VMEM((groups, seq, LANES), F32)],
        compiler_params=_params("parallel", "parallel"),
        name="dilated",
    )(*([p_dil] * len(in_specs)))


def _diff_kernel(q1_ref, q2_ref, k1_ref, k2_ref, v_ref, lq1_ref, lk1_ref, lq2_ref, lk2_ref, g_ref,
                 o_ref, *, tq, npairs, lam_init):
    i = pl.program_id(2)
    lane = lax.broadcasted_iota(jnp.int32, (1, LANES), 1)
    halves = (lane < HEAD_DIM, lane >= HEAD_DIM)
    heads = [(pp, h) for pp in range(npairs) for h in range(2)]
    chains = [(hd, t) for hd in range(len(heads)) for t in range(2)]
    q_refs = (q1_ref, q2_ref)
    k_refs = (k1_ref, k2_ref)
    pair_cols = [slice(pp * LANES, (pp + 1) * LANES) for pp in range(npairs)]
    head_cols = [slice(hd * DIFF_V_DIM, (hd + 1) * DIFF_V_DIM) for hd in range(len(heads))]
    q_pairs = [[r[:, cols] for cols in pair_cols] for r in q_refs]
    qs = [jnp.where(halves[heads[hd][1]], q_pairs[t][heads[hd][0]], jnp.zeros_like(q_pairs[t][heads[hd][0]]))
          for hd, t in chains]
    key = lax.broadcasted_iota(jnp.int32, (tq, tq), 0)
    qry = lax.broadcasted_iota(jnp.int32, (tq, tq), 1)
    causal = key <= qry

    def block(j, state, masked):
        start = pl.multiple_of(j * tq, tq)
        kbs = [[r[pl.ds(start, tq), cols] for cols in pair_cols] for r in k_refs]
        vbs = [v_ref[pl.ds(start, tq), cols] for cols in head_cols]
        s = [_nt_dot(kbs[t][heads[hd][0]], q) for (hd, t), q in zip(chains, qs)]
        if masked:
            s = [jnp.where(causal, x, -jnp.inf) for x in s]
        m_new = [jnp.maximum(st[0], jnp.max(x, axis=0, keepdims=True)) for st, x in zip(state, s)]
        pb, psum = [], []
        for x, mx in zip(s, m_new):
            p = jnp.exp(x - mx)
            pb.append(p.astype(BF16))
            psum.append(jnp.sum(p, axis=0, keepdims=True))
        pv = [_tn_dot(vbs[hd], jnp.concatenate(pb[2 * hd:2 * hd + 2], axis=1)) for hd in range(len(heads))]
        pv = [pv[hd][:, t * tq:(t + 1) * tq] for hd, t in chains]
        alpha = [jnp.exp(st[0] - mx) for st, mx in zip(state, m_new)]
        l_new = [a * st[1] + y for a, st, y in zip(alpha, state, psum)]
        acc = [a * st[2] + x for a, st, x in zip(alpha, state, pv)]
        return tuple(zip(m_new, l_new, acc))

    init = (jnp.full((1, tq), -jnp.inf, F32), jnp.zeros((1, tq), F32), jnp.zeros((DIFF_V_DIM, tq), F32))
    state = block(i, (init,) * len(chains), True)
    state = lax.fori_loop(0, i, lambda jj, st: block(i - 1 - jj, st, False), state)

    lam = (jnp.exp(jnp.sum(lq1_ref[...] * lk1_ref[...], axis=-1, keepdims=True))
           - jnp.exp(jnp.sum(lq2_ref[...] * lk2_ref[...], axis=-1, keepdims=True)) + lam_init)
    for hd, cols in enumerate(head_cols):
        (_, l1, a1), (_, l2, a2) = state[2 * hd], state[2 * hd + 1]
        o = (a1 * (1.0 / l1) - lam * (a2 * (1.0 / l2))).T
        o_ref[:, cols] = (_rmsnorm(o, g_ref[...], SUBLN_EPS) * (1.0 - lam_init)).astype(o_ref.dtype)


def _diff_attention(p_rope, p_plain, lam_vecs, subln, batch, seq, lam_init):
    tq = 256
    npairs = 2
    nq = seq // tq
    qk_w = npairs * LANES
    v_w = 2 * npairs * DIFF_V_DIM
    groups = DIFF_QK_WIDTH // qk_w
    q1c, q2c, k1c, k2c = (c // qk_w for c in (R_Q1, R_Q2, R_K1, R_K2))
    vcol = P_VC // v_w
    qspec = lambda c0: pl.BlockSpec((tq, qk_w), lambda b, p, i: (b * nq + i, c0 + p))
    kspec = lambda c0: pl.BlockSpec((seq, qk_w), lambda b, p, i: (b, c0 + p))
    vec = lambda w: pl.BlockSpec((1, w), lambda b, p, i: (0, 0))
    return pl.pallas_call(
        functools.partial(_diff_kernel, tq=tq, npairs=npairs, lam_init=lam_init),
        grid=(batch, groups, nq),
        in_specs=[qspec(q1c), qspec(q2c), kspec(k1c), kspec(k2c),
                  pl.BlockSpec((seq, v_w), lambda b, p, i: (b, vcol + p)),
                  vec(HEAD_DIM), vec(HEAD_DIM), vec(HEAD_DIM), vec(HEAD_DIM), vec(DIFF_V_DIM)],
        out_specs=pl.BlockSpec((tq, v_w), lambda b, p, i: (b * nq + i, p)),
        out_shape=jax.ShapeDtypeStruct((batch * seq, DIFF_V_WIDTH), BF16),
        compiler_params=_params("parallel", "parallel", "arbitrary"),
        name="diff_attention",
    )(p_rope, p_rope, p_rope, p_rope, p_plain, *lam_vecs, subln)


def _mix_kernel(x_ref, g_ref, wg_ref, bg_ref, ya_ref, yb_ref, yc_ref,
                wa_ref, wb_ref, wc_ref, wo_ref, out_ref):
    x = x_ref[...]
    hn = _rmsnorm(x, g_ref[...], NORM_EPS).astype(BF16)
    gates = jax.nn.sigmoid(jnp.dot(hn, wg_ref[...], preferred_element_type=F32) + bg_ref[...])
    up_a = jnp.dot(ya_ref[...], wa_ref[...], preferred_element_type=F32)
    up_b = jnp.dot(yb_ref[...], wb_ref[...], preferred_element_type=F32)
    up_c = jnp.dot(yc_ref[...], wc_ref[...], preferred_element_type=F32)
    d = D_MODEL
    merged = gates[:, :d] * up_a + gates[:, d:2 * d] * up_b + gates[:, 2 * d:] * up_c
    out_ref[...] = x + jnp.dot(merged.astype(BF16), wo_ref[...], preferred_element_type=F32)


def _mix_out(x, gain, w_gate, b_gate, y_a, y_b, y_c, w_up_a, w_up_b, w_up_c, w_out):
    rows = x.shape[0]
    tm = 512
    row = lambda w: pl.BlockSpec((tm, w), lambda i: (i, 0))
    full = lambda a: _resident(a.shape, lambda i: (0, 0))
    return pl.pallas_call(
        _mix_kernel,
        grid=(rows // tm,),
        in_specs=[row(D_MODEL), full(gain), full(w_gate), full(b_gate),
                  row(SB_WIDTH), row(DIL_GROUP_WIDTH), row(DIFF_V_WIDTH),
                  full(w_up_a), full(w_up_b), full(w_up_c), full(w_out)],
        out_specs=row(D_MODEL),
        out_shape=jax.ShapeDtypeStruct((rows, D_MODEL), F32),
        compiler_params=_params("parallel"),
        name="mix_out",
    )(x, gain, w_gate, b_gate, y_a, y_b, y_c, w_up_a, w_up_b, w_up_c, w_out)


def _column_scale(width, scaled_ranges):
    col = jnp.arange(width)
    s = jnp.ones((width,), F32)
    for lo, hi in scaled_ranges:
        s = jnp.where((col >= lo) & (col < hi), QK_SCALE, s)
    return s.reshape(1, width)


def kernel(x, positions, ffn1_norm, ffn1_w_in, ffn1_w_out, mix_norm, w_in, b_gate, lam_q1, lam_k1, lam_q2, lam_k2, diff_subln, w_up_a, w_up_b, w_up_c, w_out, ffn2_norm, ffn2_w_in, ffn2_w_out, final_norm):
    batch, seq, d = x.shape
    rows = batch * seq
    x = x.reshape(rows, d)
    tables = _rope_tables(positions)
    vec = lambda t: t.reshape(1, -1)

    o_qa, o_ka, o_va = 0, SB_WIDTH, 2 * SB_WIDTH
    o_qb = 3 * SB_WIDTH
    o_kb, o_vb = o_qb + DIL_WIDTH, o_qb + 2 * DIL_WIDTH
    o_q1 = o_qb + 3 * DIL_WIDTH
    o_vc = o_q1 + 4 * DIFF_QK_WIDTH
    o_gate = o_vc + DIFF_V_WIDTH
    o_qa_new = ROTARY_COLS + DIL_WIDTH
    scale = _column_scale(QKV_COLS, [(0, DIL_WIDTH),
                                     (2 * DIL_WIDTH, 2 * DIL_WIDTH + 2 * DIFF_QK_WIDTH),
                                     (o_qa_new, o_qa_new + SB_WIDTH)])

    for l in range(DEPTH):
        wl = w_in[l]
        w_qkv = jnp.concatenate([wl[:, o_qb:o_vb], wl[:, o_q1:o_vc], wl[:, o_vb:o_q1],
                                 wl[:, o_qa:o_qb], wl[:, o_vc:o_gate]], axis=1).astype(BF16)
        w_gate = wl[:, o_gate:].astype(BF16)

        x = _ffn(x, vec(ffn1_norm[l]), ffn1_w_in[l].astype(BF16), ffn1_w_out[l].astype(BF16),
                 vec(final_norm), False)

        gain = vec(mix_norm[l])
        p_dil, p_rope, p_plain = _proj(x, gain, w_qkv, scale, tables)

        y_a = _stick_breaking(p_plain, batch, seq)

        y_b = _dilated(p_dil, batch, seq)

        lam_init = 0.8 - 0.6 * math.exp(-0.3 * l)
        y_c = _diff_attention(p_rope, p_plain,
                              [vec(t[l]) for t in (lam_q1, lam_k1, lam_q2, lam_k2)],
                              vec(diff_subln[l]), batch, seq, lam_init)

        x = _mix_out(x, gain, w_gate, vec(b_gate[l]), y_a, y_b, y_c,
                     w_up_a[l].astype(BF16), w_up_b[l].astype(BF16), w_up_c[l].astype(BF16),
                     w_out[l].astype(BF16))

        x = _ffn(x, vec(ffn2_norm[l]), ffn2_w_in[l].astype(BF16), ffn2_w_out[l].astype(BF16),
                 vec(final_norm), l == DEPTH - 1)
    return x.reshape(batch, seq, d)
```

```python
import functools
import math

import jax
import jax.numpy as jnp
from jax import lax
from jax.experimental import pallas as pl
from jax.experimental.pallas import tpu as pltpu

F32 = jnp.float32
BF16 = jnp.bfloat16

D_MODEL = 1024
DEPTH = 2
HEAD_DIM = 64
ROT_DIM = HEAD_DIM // 4
ROT_HALF = ROT_DIM // 2
ROPE_THETA = 500000.0
NORM_EPS = 1e-6
SUBLN_EPS = 1e-5
QK_SCALE = HEAD_DIM ** -0.5

SB_HEADS = 8
SB_WIDTH = SB_HEADS * HEAD_DIM
DIL_PATTERNS = ((128, 1), (512, 4), (2048, 16))
DIL_SPAN = 128
DIL_GROUP_WIDTH = 4 * HEAD_DIM
DIL_WIDTH = 3 * DIL_GROUP_WIDTH
DIFF_HEADS = 4
DIFF_QK_WIDTH = DIFF_HEADS * HEAD_DIM
DIFF_V_DIM = 2 * HEAD_DIM
DIFF_V_WIDTH = DIFF_HEADS * DIFF_V_DIM
GATE_WIDTH = 3 * D_MODEL
D_FF = 2816

SB_NEGLIGIBLE = 104.0
LANES = 128
VMEM_LIMIT = 56 * 1024 * 1024

ROTARY_COLS = 2 * DIL_WIDTH + 4 * DIFF_QK_WIDTH
QKV_COLS = ROTARY_COLS + DIL_WIDTH + 3 * SB_WIDTH + DIFF_V_WIDTH
DIL_COLS = 3 * DIL_WIDTH
D_QB, D_KB, D_VB = 0, DIL_WIDTH, 2 * DIL_WIDTH
ROPE_COLS = 4 * DIFF_QK_WIDTH
R_Q1, R_Q2, R_K1, R_K2 = (t * DIFF_QK_WIDTH for t in range(4))
PLAIN_COLS = 3 * SB_WIDTH + DIFF_V_WIDTH
P_QA, P_KA, P_VA, P_VC = 0, SB_WIDTH, 2 * SB_WIDTH, 3 * SB_WIDTH


def _params(*sem):
    return pltpu.CompilerParams(dimension_semantics=sem, vmem_limit_bytes=VMEM_LIMIT)


def _rmsnorm(x, g, eps):
    ms = jnp.mean(x * x, axis=-1, keepdims=True)
    return x * lax.rsqrt(ms + eps) * g


def _rope_table_kernel(pos_ref, invf_ref, cos_ref, sin_ref):
    ang = pos_ref[...].astype(F32) * invf_ref[...]
    lane = lax.broadcasted_iota(jnp.int32, ang.shape, 1) % HEAD_DIM
    c = jnp.cos(ang)
    s = jnp.sin(ang)
    cos_ref[...] = jnp.where(lane < ROT_DIM, c, 1.0)
    sin_ref[...] = jnp.where(lane < ROT_HALF, -s, jnp.where(lane < ROT_DIM, s, 0.0))


def _rope_tables(positions):
    rows = positions.size
    tm = 1024
    inv_freq = ROPE_THETA ** (-jnp.arange(0, ROT_DIM, 2, dtype=F32) / ROT_DIM)
    lane = jnp.arange(LANES) % HEAD_DIM
    invf = jnp.where(lane < ROT_DIM, inv_freq[lane % ROT_HALF], 0.0).reshape(1, LANES)
    return pl.pallas_call(
        _rope_table_kernel,
        grid=(rows // tm,),
        in_specs=[pl.BlockSpec((tm, 1), lambda i: (i, 0)),
                  pl.BlockSpec((1, LANES), lambda i: (0, 0))],
        out_specs=[pl.BlockSpec((tm, LANES), lambda i: (i, 0))] * 2,
        out_shape=[jax.ShapeDtypeStruct((rows, LANES), F32)] * 2,
        compiler_params=_params("parallel"),
        name="rope_tables",
    )(positions.reshape(rows, 1), invf)


def _ffn_kernel(x_ref, g_ref, wg_ref, wu_ref, wo_ref, fg_ref, o_ref, *, final):
    x = x_ref[...]
    n = _rmsnorm(x, g_ref[...], NORM_EPS).astype(BF16)
    gate = jnp.dot(n, wg_ref[...], preferred_element_type=F32)
    up = jnp.dot(n, wu_ref[...], preferred_element_type=F32)
    h = (gate * jax.nn.sigmoid(gate) * up).astype(BF16)
    y = x + 0.5 * jnp.dot(h, wo_ref[...], preferred_element_type=F32)
    if final:
        y = _rmsnorm(y, fg_ref[...], NORM_EPS)
    o_ref[...] = y


def _resident(shape, index_map):
    return pl.BlockSpec(shape, index_map, pipeline_mode=pl.Buffered(1))


def _ffn(x, gain, w_in, w_out, final_gain, final):
    rows = x.shape[0]
    tm = 512
    return pl.pallas_call(
        functools.partial(_ffn_kernel, final=final),
        grid=(rows // tm,),
        in_specs=[pl.BlockSpec((tm, D_MODEL), lambda i: (i, 0)),
                  _resident((1, D_MODEL), lambda i: (0, 0)),
                  _resident((D_MODEL, D_FF), lambda i: (0, 0)),
                  _resident((D_MODEL, D_FF), lambda i: (0, 1)),
                  _resident((D_FF, D_MODEL), lambda i: (0, 0)),
                  _resident((1, D_MODEL), lambda i: (0, 0))],
        out_specs=pl.BlockSpec((tm, D_MODEL), lambda i: (i, 0)),
        out_shape=jax.ShapeDtypeStruct((rows, D_MODEL), F32),
        compiler_params=_params("parallel"),
        name="ffn_final" if final else "ffn",
    )(x, gain, w_in, w_in, w_out, final_gain)


def _proj_kernel(x_ref, g_ref, w_ref, sc_ref, cos_ref, sin_ref, dil_ref, rope_ref, plain_ref):
    n = _rmsnorm(x_ref[...], g_ref[...], NORM_EPS).astype(BF16)
    rot = jnp.dot(n, w_ref[:, :ROTARY_COLS], preferred_element_type=F32) * sc_ref[:, :ROTARY_COLS]
    rest = jnp.dot(n, w_ref[:, ROTARY_COLS:], preferred_element_type=F32) * sc_ref[:, ROTARY_COLS:]
    cos = cos_ref[...]
    sin = sin_ref[...]
    first_half = lax.broadcasted_iota(jnp.int32, cos.shape, 1) % HEAD_DIM < ROT_HALF
    for c in range(ROTARY_COLS // LANES):
        t = rot[:, c * LANES:(c + 1) * LANES]
        partner = jnp.where(first_half,
                            pltpu.roll(t, LANES - ROT_HALF, axis=1),
                            pltpu.roll(t, ROT_HALF, axis=1))
        roped = t * cos + partner * sin
        lo = c * LANES
        if lo < D_VB:
            dil_ref[:, lo:lo + LANES] = roped
        else:
            rope_ref[:, lo - D_VB:lo - D_VB + LANES] = roped.astype(rope_ref.dtype)
    dil_ref[:, D_VB:] = rest[:, :DIL_WIDTH]
    plain_ref[...] = rest[:, DIL_WIDTH:].astype(plain_ref.dtype)


def _proj(x, gain, w, scale, tables):
    rows = x.shape[0]
    tm = 512
    row = lambda width: pl.BlockSpec((tm, width), lambda i: (i, 0))
    const = lambda a: _resident(a.shape, lambda i: (0, 0))
    return pl.pallas_call(
        _proj_kernel,
        grid=(rows // tm,),
        in_specs=[row(D_MODEL), const(gain), const(w), const(scale), row(LANES), row(LANES)],
        out_specs=[row(DIL_COLS), row(ROPE_COLS), row(PLAIN_COLS)],
        out_shape=[jax.ShapeDtypeStruct((rows, DIL_COLS), F32),
                   jax.ShapeDtypeStruct((rows, ROPE_COLS), BF16),
                   jax.ShapeDtypeStruct((rows, PLAIN_COLS), BF16)],
        compiler_params=_params("parallel"),
        name="proj",
    )(x, gain, w, scale, *tables)


def _nt_dot(a, b):
    return lax.dot_general(a, b, (((1,), (1,)), ((), ())), preferred_element_type=F32)


def _tn_dot(a, b):
    return lax.dot_general(a, b, (((0,), (0,)), ((), ())), preferred_element_type=F32)


def _sb_kernel(q_ref, k_ref, v_ref, o_ref, *, tq, npairs):
    i = pl.program_id(2)
    lane = lax.broadcasted_iota(jnp.int32, (1, LANES), 1)
    halves = (lane < HEAD_DIM, lane >= HEAD_DIM)
    chains = [(pp, h) for pp in range(npairs) for h in range(2)]
    pair_cols = [slice(pp * LANES, (pp + 1) * LANES) for pp in range(npairs)]
    q_pairs = [q_ref[:, cols] for cols in pair_cols]
    qms = [jnp.where(halves[h], q_pairs[pp], jnp.zeros_like(q_pairs[pp])) for pp, h in chains]
    key = lax.broadcasted_iota(jnp.int32, (tq, tq), 0)
    qry = lax.broadcasted_iota(jnp.int32, (tq, tq), 1)
    later_key = (qry > key).astype(BF16)
    strict = key < qry

    def block(j, state, masked):
        start = pl.multiple_of(j * tq, tq)
        kbs = [k_ref[pl.ds(start, tq), cols] for cols in pair_cols]
        vbs = [v_ref[pl.ds(start, tq), cols] for cols in pair_cols]
        z = [_nt_dot(kbs[pp], qm) for (pp, h), qm in zip(chains, qms)]
        log_beta, rounded, sums = [], [], []
        for x in z:
            u = jnp.maximum(x, 0.0) + jnp.log(1.0 + jnp.exp(-jnp.abs(x)))
            if masked:
                u = jnp.where(strict, u, 0.0)
            rounded.append(u.astype(BF16))
            log_beta.append(x - u)
            sums.append(jnp.sum(u, axis=0, keepdims=True))
        parts = jnp.dot(later_key, jnp.concatenate(rounded, axis=1), preferred_element_type=F32)
        w = []
        for c, (lb, st) in enumerate(zip(log_beta, state)):
            after = parts[:, c * tq:(c + 1) * tq] + st[1]
            wc = jnp.exp(lb - after)
            if masked:
                wc = jnp.where(strict, wc, 0.0)
            w.append(wc.astype(BF16))
        pv = [_tn_dot(vbs[pp], jnp.concatenate(w[2 * pp:2 * pp + 2], axis=1)) for pp in range(npairs)]
        return tuple((st[0] + pv[pp][:, h * tq:(h + 1) * tq], st[1] + y)
                     for (pp, h), st, y in zip(chains, state, sums))

    def smallest_carry(state):
        low = state[0][1]
        for st in state[1:]:
            low = jnp.minimum(low, st[1])
        return jnp.min(low)

    def more(c):
        j, _, low = c
        return (j >= 0) & (low < SB_NEGLIGIBLE)

    def step(c):
        j, st, _ = c
        st = block(j, st, False)
        return j - 1, st, smallest_carry(st)

    zero = (jnp.zeros((LANES, tq), F32), jnp.zeros((1, tq), F32))
    state = block(i, (zero,) * len(chains), True)
    _, state, _ = lax.while_loop(more, step, (i - 1, state, smallest_carry(state)))
    feat = lax.broadcasted_iota(jnp.int32, (LANES, 1), 0)
    for pp, cols in enumerate(pair_cols):
        o_ref[:, cols] = jnp.where(feat < HEAD_DIM, state[2 * pp][0], state[2 * pp + 1][0]).T.astype(o_ref.dtype)


def _stick_breaking(p_plain, batch, seq):
    tq = 256
    npairs = 4
    nq = seq // tq
    width = npairs * LANES
    groups = SB_WIDTH // width
    kcol, vcol = P_KA // width, P_VA // width
    return pl.pallas_call(
        functools.partial(_sb_kernel, tq=tq, npairs=npairs),
        grid=(batch, groups, nq),
        in_specs=[pl.BlockSpec((tq, width), lambda b, p, i: (b * nq + i, p)),
                  pl.BlockSpec((seq, width), lambda b, p, i: (b, kcol + p)),
                  pl.BlockSpec((seq, width), lambda b, p, i: (b, vcol + p))],
        out_specs=pl.BlockSpec((tq, width), lambda b, p, i: (b * nq + i, p)),
        out_shape=jax.ShapeDtypeStruct((batch * seq, SB_WIDTH), BF16),
        compiler_params=_params("parallel", "parallel", "arbitrary"),
        name="stick_breaking",
    )(p_plain, p_plain, p_plain)


def _rows_of(start, size, stride):
    return pl.ds(start, size) if stride == 1 else pl.ds(start, size, stride=stride)


def _dil_kernel(*refs):
    qkv_refs, (y_ref, o_scr, lse_scr) = refs[:-3], refs[-3:]
    span = DIL_SPAN
    seq = y_ref.shape[0]
    lane = lax.broadcasted_iota(jnp.int32, (1, LANES), 1)
    heads = [(lane >= h * HEAD_DIM) & (lane < (h + 1) * HEAD_DIM) for h in range(LANES // HEAD_DIM)]

    def band(nkeys):
        qi = lax.broadcasted_iota(jnp.int32, (span, nkeys), 0) + (nkeys - span)
        kj = lax.broadcasted_iota(jnp.int32, (span, nkeys), 1)
        return (qi - kj >= 0) & (qi - kj <= span)

    bands = {span: band(span), 2 * span: band(2 * span)}

    for g, (window, dil) in enumerate(DIL_PATTERNS):
        q_ref, k_ref, v_ref = qkv_refs[3 * g:3 * g + 3]

        def tile_rows(r, c):
            return _rows_of(r * span * dil + c, span, dil)

        def keys_of(ref, r, c):
            own = ref[tile_rows(r, c), :]
            if r == 0:
                return own.astype(BF16)
            return jnp.concatenate([ref[tile_rows(r - 1, c), :], own], axis=0).astype(BF16)

        all_tiles = [(r, c) for r in range(seq // (span * dil)) for c in range(dil)]
        per_pass = 8
        for t0 in range(0, len(all_tiles), per_pass):
            tiles = all_tiles[t0:t0 + per_pass]
            chains = [(t, h) for t in range(len(tiles)) for h in range(len(heads))]
            qs = [q_ref[tile_rows(r, c), :].astype(BF16) for r, c in tiles]
            ks = [keys_of(k_ref, r, c) for r, c in tiles]
            vs = [keys_of(v_ref, r, c) for r, c in tiles]
            s = [_nt_dot(jnp.where(heads[h], qs[t], jnp.zeros_like(qs[t])), ks[t]) for t, h in chains]
            s = [jnp.where(bands[x.shape[1]], x, -jnp.inf) for x in s]
            m = [jnp.max(x, axis=-1, keepdims=True) for x in s]
            p = [jnp.exp(x - mx) for x, mx in zip(s, m)]
            l = [jnp.sum(x, axis=-1, keepdims=True) for x in p]
            o = [jnp.dot(x.astype(BF16), vs[t], preferred_element_type=F32) for x, (t, h) in zip(p, chains)]
            o = [x * (1.0 / lx) for x, lx in zip(o, l)]
            lse = [mx + jnp.log(lx) for mx, lx in zip(m, l)]
            for t, (r, c) in enumerate(tiles):
                out = jnp.zeros((span, LANES), F32)
                lse_out = jnp.zeros((span, LANES), F32)
                for h in range(len(heads)):
                    out = jnp.where(heads[h], o[t * len(heads) + h], out)
                    lse_out = jnp.where(heads[h], lse[t * len(heads) + h], lse_out)
                o_scr[g, tile_rows(r, c), :] = out
                lse_scr[g, tile_rows(r, c), :] = lse_out

    chunk = 2 * span
    for r0 in range(0, seq, chunk):
        rows = slice(r0, r0 + chunk)
        lses = [lse_scr[g, rows, :] for g in range(len(DIL_PATTERNS))]
        top = functools.reduce(jnp.maximum, lses)
        es = [jnp.exp(x - top) for x in lses]
        inv = 1.0 / functools.reduce(lambda a, b: a + b, es)
        y = functools.reduce(lambda a, b: a + b, [(e * inv) * o_scr[g, rows, :] for g, e in enumerate(es)])
        y_ref[rows, :] = y.astype(y_ref.dtype)


def _dilated(p_dil, batch, seq):
    pairs = DIL_GROUP_WIDTH // LANES
    groups = len(DIL_PATTERNS)
    assert all(window // dil == DIL_SPAN and seq % (DIL_SPAN * dil) == 0 for window, dil in DIL_PATTERNS)
    spec = lambda col: pl.BlockSpec((seq, LANES), lambda b, u: (b, col + u))
    in_specs = [spec(c // LANES + g * pairs) for g in range(groups) for c in (D_QB, D_KB, D_VB)]
    return pl.pallas_call(
        _dil_kernel,
        grid=(batch, pairs),
        in_specs=in_specs,
        out_specs=spec(0),
        out_shape=jax.ShapeDtypeStruct((batch * seq, DIL_GROUP_WIDTH), BF16),
        scratch_shapes=[pltpu.VMEM((groups, seq, LANES), F32), pltpu.VMEM((groups, seq, LANES), F32)],
        compiler_params=_params("parallel", "parallel"),
        name="dilated",
    )(*([p_dil] * len(in_specs)))


def _diff_kernel(q1_ref, q2_ref, k1_ref, k2_ref, v_ref, lq1_ref, lk1_ref, lq2_ref, lk2_ref, g_ref,
                 o_ref, *, tq, npairs, lam_init):
    i = pl.program_id(2)
    lane = lax.broadcasted_iota(jnp.int32, (1, LANES), 1)
    halves = (lane < HEAD_DIM, lane >= HEAD_DIM)
    heads = [(pp, h) for pp in range(npairs) for h in range(2)]
    chains = [(hd, t) for hd in range(len(heads)) for t in range(2)]
    q_refs = (q1_ref, q2_ref)
    k_refs = (k1_ref, k2_ref)
    pair_cols = [slice(pp * LANES, (pp + 1) * LANES) for pp in range(npairs)]
    head_cols = [slice(hd * DIFF_V_DIM, (hd + 1) * DIFF_V_DIM) for hd in range(len(heads))]
    q_pairs = [[r[:, cols] for cols in pair_cols] for r in q_refs]
    qs = [jnp.where(halves[heads[hd][1]], q_pairs[t][heads[hd][0]], jnp.zeros_like(q_pairs[t][heads[hd][0]]))
          for hd, t in chains]
    key = lax.broadcasted_iota(jnp.int32, (tq, tq), 0)
    qry = lax.broadcasted_iota(jnp.int32, (tq, tq), 1)
    causal = key <= qry

    def scores(j, nk):
        start = pl.multiple_of(j * tq, tq)
        kbs = [[r[pl.ds(start, nk), cols] for cols in pair_cols] for r in k_refs]
        return [_nt_dot(kbs[t][heads[hd][0]], q) for (hd, t), q in zip(chains, qs)]

    def absorb(s, j, nk, state, masked):
        start = pl.multiple_of(j * tq, tq)
        vbs = [v_ref[pl.ds(start, nk), cols] for cols in head_cols]
        if masked:
            s = [jnp.where(causal, x, -jnp.inf) for x in s]
        m_new = [jnp.maximum(st[0], jnp.max(x, axis=0, keepdims=True)) for st, x in zip(state, s)]
        pb, psum = [], []
        for x, mx in zip(s, m_new):
            p = jnp.exp(x - mx)
            pb.append(p.astype(BF16))
            psum.append(jnp.sum(p, axis=0, keepdims=True))
        pv = [_tn_dot(vbs[hd], jnp.concatenate(pb[2 * hd:2 * hd + 2], axis=1)) for hd in range(len(heads))]
        pv = [pv[hd][:, t * tq:(t + 1) * tq] for hd, t in chains]
        alpha = [jnp.exp(st[0] - mx) for st, mx in zip(state, m_new)]
        l_new = [a * st[1] + y for a, st, y in zip(alpha, state, psum)]
        acc = [a * st[2] + x for a, st, x in zip(alpha, state, pv)]
        return tuple(zip(m_new, l_new, acc))

    init = (jnp.full((1, tq), -jnp.inf, F32), jnp.zeros((1, tq), F32), jnp.zeros((DIFF_V_DIM, tq), F32))
    state = absorb(scores(i, tq), i, tq, (init,) * len(chains), True)
    state = lax.cond(i % 2 == 1, lambda st: absorb(scores(i - 1, tq), i - 1, tq, st, False),
                     lambda st: st, state)
    state = lax.fori_loop(0, i // 2,
                          lambda jj, st: absorb(scores(2 * jj, 2 * tq), 2 * jj, 2 * tq, st, False), state)

    lam = (jnp.exp(jnp.sum(lq1_ref[...] * lk1_ref[...], axis=-1, keepdims=True))
           - jnp.exp(jnp.sum(lq2_ref[...] * lk2_ref[...], axis=-1, keepdims=True)) + lam_init)
    for hd, cols in enumerate(head_cols):
        (_, l1, a1), (_, l2, a2) = state[2 * hd], state[2 * hd + 1]
        o = (a1 * (1.0 / l1) - lam * (a2 * (1.0 / l2))).T
        o_ref[:, cols] = (_rmsnorm(o, g_ref[...], SUBLN_EPS) * (1.0 - lam_init)).astype(o_ref.dtype)


def _diff_attention(p_rope, p_plain, lam_vecs, subln, batch, seq, lam_init):
    tq = 256
    npairs = 2
    nq = seq // tq
    qk_w = npairs * LANES
    v_w = 2 * npairs * DIFF_V_DIM
    groups = DIFF_QK_WIDTH // qk_w
    q1c, q2c, k1c, k2c = (c // qk_w for c in (R_Q1, R_Q2, R_K1, R_K2))
    vcol = P_VC // v_w
    qspec = lambda c0: pl.BlockSpec((tq, qk_w), lambda b, p, i: (b * nq + i, c0 + p))
    kspec = lambda c0: pl.BlockSpec((seq, qk_w), lambda b, p, i: (b, c0 + p))
    vec = lambda w: pl.BlockSpec((1, w), lambda b, p, i: (0, 0))
    return pl.pallas_call(
        functools.partial(_diff_kernel, tq=tq, npairs=npairs, lam_init=lam_init),
        grid=(batch, groups, nq),
        in_specs=[qspec(q1c), qspec(q2c), kspec(k1c), kspec(k2c),
                  pl.BlockSpec((seq, v_w), lambda b, p, i: (b, vcol + p)),
                  vec(HEAD_DIM), vec(HEAD_DIM), vec(HEAD_DIM), vec(HEAD_DIM), vec(DIFF_V_DIM)],
        out_specs=pl.BlockSpec((tq, v_w), lambda b, p, i: (b * nq + i, p)),
        out_shape=jax.ShapeDtypeStruct((batch * seq, DIFF_V_WIDTH), BF16),
        compiler_params=_params("parallel", "parallel", "arbitrary"),
        name="diff_attention",
    )(p_rope, p_rope, p_rope, p_rope, p_plain, *lam_vecs, subln)


def _mix_kernel(x_ref, g_ref, wg_ref, bg_ref, ya_ref, yb_ref, yc_ref,
                wa_ref, wb_ref, wc_ref, wo_ref, out_ref):
    x = x_ref[...]
    hn = _rmsnorm(x, g_ref[...], NORM_EPS).astype(BF16)
    gates = jax.nn.sigmoid(jnp.dot(hn, wg_ref[...], preferred_element_type=F32) + bg_ref[...])
    up_a = jnp.dot(ya_ref[...], wa_ref[...], preferred_element_type=F32)
    up_b = jnp.dot(yb_ref[...], wb_ref[...], preferred_element_type=F32)
    up_c = jnp.dot(yc_ref[...], wc_ref[...], preferred_element_type=F32)
    d = D_MODEL
    merged = gates[:, :d] * up_a + gates[:, d:2 * d] * up_b + gates[:, 2 * d:] * up_c
    out_ref[...] = x + jnp.dot(merged.astype(BF16), wo_ref[...], preferred_element_type=F32)


def _mix_out(x, gain, w_gate, b_gate, y_a, y_b, y_c, w_up_a, w_up_b, w_up_c, w_out):
    rows = x.shape[0]
    tm = 512
    row = lambda w: pl.BlockSpec((tm, w), lambda i: (i, 0))
    full = lambda a: _resident(a.shape, lambda i: (0, 0))
    return pl.pallas_call(
        _mix_kernel,
        grid=(rows // tm,),
        in_specs=[row(D_MODEL), full(gain), full(w_gate), full(b_gate),
                  row(SB_WIDTH), row(DIL_GROUP_WIDTH), row(DIFF_V_WIDTH),
                  full(w_up_a), full(w_up_b), full(w_up_c), full(w_out)],
        out_specs=row(D_MODEL),
        out_shape=jax.ShapeDtypeStruct((rows, D_MODEL), F32),
        compiler_params=_params("parallel"),
        name="mix_out",
    )(x, gain, w_gate, b_gate, y_a, y_b, y_c, w_up_a, w_up_b, w_up_c, w_out)


def _column_scale(width, scaled_ranges):
    col = jnp.arange(width)
    s = jnp.ones((width,), F32)
    for lo, hi in scaled_ranges:
        s = jnp.where((col >= lo) & (col < hi), QK_SCALE, s)
    return s.reshape(1, width)


def kernel(x, positions, ffn1_norm, ffn1_w_in, ffn1_w_out, mix_norm, w_in, b_gate, lam_q1, lam_k1, lam_q2, lam_k2, diff_subln, w_up_a, w_up_b, w_up_c, w_out, ffn2_norm, ffn2_w_in, ffn2_w_out, final_norm):
    batch, seq, d = x.shape
    rows = batch * seq
    x = x.reshape(rows, d)
    tables = _rope_tables(positions)
    vec = lambda t: t.reshape(1, -1)

    o_qa, o_ka, o_va = 0, SB_WIDTH, 2 * SB_WIDTH
    o_qb = 3 * SB_WIDTH
    o_kb, o_vb = o_qb + DIL_WIDTH, o_qb + 2 * DIL_WIDTH
    o_q1 = o_qb + 3 * DIL_WIDTH
    o_vc = o_q1 + 4 * DIFF_QK_WIDTH
    o_gate = o_vc + DIFF_V_WIDTH
    o_qa_new = ROTARY_COLS + DIL_WIDTH
    scale = _column_scale(QKV_COLS, [(0, DIL_WIDTH),
                                     (2 * DIL_WIDTH, 2 * DIL_WIDTH + 2 * DIFF_QK_WIDTH),
                                     (o_qa_new, o_qa_new + SB_WIDTH)])

    for l in range(DEPTH):
        wl = w_in[l]
        w_qkv = jnp.concatenate([wl[:, o_qb:o_vb], wl[:, o_q1:o_vc], wl[:, o_vb:o_q1],
                                 wl[:, o_qa:o_qb], wl[:, o_vc:o_gate]], axis=1).astype(BF16)
        w_gate = wl[:, o_gate:].astype(BF16)

        x = _ffn(x, vec(ffn1_norm[l]), ffn1_w_in[l].astype(BF16), ffn1_w_out[l].astype(BF16),
                 vec(final_norm), False)

        gain = vec(mix_norm[l])
        p_dil, p_rope, p_plain = _proj(x, gain, w_qkv, scale, tables)

        y_a = _stick_breaking(p_plain, batch, seq)

        y_b = _dilated(p_dil, batch, seq)

        lam_init = 0.8 - 0.6 * math.exp(-0.3 * l)
        y_c = _diff_attention(p_rope, p_plain,
                              [vec(t[l]) for t in (lam_q1, lam_k1, lam_q2, lam_k2)],
                              vec(diff_subln[l]), batch, seq, lam_init)

        x = _mix_out(x, gain, w_gate, vec(b_gate[l]), y_a, y_b, y_c,
                     w_up_a[l].astype(BF16), w_up_b[l].astype(BF16), w_up_c[l].astype(BF16),
                     w_out[l].astype(BF16))

        x = _ffn(x, vec(ffn2_norm[l]), ffn2_w_in[l].astype(BF16), ffn2_w_out[l].astype(BF16),
                 vec(final_norm), l == DEPTH - 1)
    return x.reshape(batch, seq, d)
```

```python
import functools
import math

import jax
import jax.numpy as jnp
from jax import lax
from jax.experimental import pallas as pl
from jax.experimental.pallas import tpu as pltpu

F32 = jnp.float32
BF16 = jnp.bfloat16

D_MODEL = 1024
DEPTH = 2
HEAD_DIM = 64
ROT_DIM = HEAD_DIM // 4
ROT_HALF = ROT_DIM // 2
ROPE_THETA = 500000.0
NORM_EPS = 1e-6
SUBLN_EPS = 1e-5
QK_SCALE = HEAD_DIM ** -0.5

SB_HEADS = 8
SB_WIDTH = SB_HEADS * HEAD_DIM
DIL_PATTERNS = ((128, 1), (512, 4), (2048, 16))
DIL_SPAN = 128
DIL_GROUP_WIDTH = 4 * HEAD_DIM
DIL_WIDTH = 3 * DIL_GROUP_WIDTH
DIFF_HEADS = 4
DIFF_QK_WIDTH = DIFF_HEADS * HEAD_DIM
DIFF_V_DIM = 2 * HEAD_DIM
DIFF_V_WIDTH = DIFF_HEADS * DIFF_V_DIM
GATE_WIDTH = 3 * D_MODEL
D_FF = 2816

SB_NEGLIGIBLE = 104.0
LANES = 128
VMEM_LIMIT = 56 * 1024 * 1024

ROTARY_COLS = 2 * DIL_WIDTH + 4 * DIFF_QK_WIDTH
QKV_COLS = ROTARY_COLS + DIL_WIDTH + 3 * SB_WIDTH + DIFF_V_WIDTH
DIL_COLS = 3 * DIL_WIDTH
D_QB, D_KB, D_VB = 0, DIL_WIDTH, 2 * DIL_WIDTH
ROPE_COLS = 4 * DIFF_QK_WIDTH
R_Q1, R_Q2, R_K1, R_K2 = (t * DIFF_QK_WIDTH for t in range(4))
PLAIN_COLS = 3 * SB_WIDTH + DIFF_V_WIDTH
P_QA, P_KA, P_VA, P_VC = 0, SB_WIDTH, 2 * SB_WIDTH, 3 * SB_WIDTH


def _params(*sem):
    return pltpu.CompilerParams(dimension_semantics=sem, vmem_limit_bytes=VMEM_LIMIT)


def _rmsnorm(x, g, eps):
    ms = jnp.mean(x * x, axis=-1, keepdims=True)
    return x * lax.rsqrt(ms + eps) * g


def _rope_table_kernel(pos_ref, invf_ref, cos_ref, sin_ref):
    ang = pos_ref[...].astype(F32) * invf_ref[...]
    lane = lax.broadcasted_iota(jnp.int32, ang.shape, 1) % HEAD_DIM
    c = jnp.cos(ang)
    s = jnp.sin(ang)
    cos_ref[...] = jnp.where(lane < ROT_DIM, c, 1.0)
    sin_ref[...] = jnp.where(lane < ROT_HALF, -s, jnp.where(lane < ROT_DIM, s, 0.0))


def _rope_tables(positions):
    rows = positions.size
    tm = 1024
    inv_freq = ROPE_THETA ** (-jnp.arange(0, ROT_DIM, 2, dtype=F32) / ROT_DIM)
    lane = jnp.arange(LANES) % HEAD_DIM
    invf = jnp.where(lane < ROT_DIM, inv_freq[lane % ROT_HALF], 0.0).reshape(1, LANES)
    return pl.pallas_call(
        _rope_table_kernel,
        grid=(rows // tm,),
        in_specs=[pl.BlockSpec((tm, 1), lambda i: (i, 0)),
                  pl.BlockSpec((1, LANES), lambda i: (0, 0))],
        out_specs=[pl.BlockSpec((tm, LANES), lambda i: (i, 0))] * 2,
        out_shape=[jax.ShapeDtypeStruct((rows, LANES), F32)] * 2,
        compiler_params=_params("parallel"),
        name="rope_tables",
    )(positions.reshape(rows, 1), invf)


def _ffn_kernel(x_ref, g_ref, wg_ref, wu_ref, wo_ref, fg_ref, o_ref, *, final):
    x = x_ref[...]
    n = _rmsnorm(x, g_ref[...], NORM_EPS).astype(BF16)
    gate = jnp.dot(n, wg_ref[...], preferred_element_type=F32)
    up = jnp.dot(n, wu_ref[...], preferred_element_type=F32)
    h = (gate * jax.nn.sigmoid(gate) * up).astype(BF16)
    y = x + 0.5 * jnp.dot(h, wo_ref[...], preferred_element_type=F32)
    if final:
        y = _rmsnorm(y, fg_ref[...], NORM_EPS)
    o_ref[...] = y


def _resident(shape, index_map):
    return pl.BlockSpec(shape, index_map, pipeline_mode=pl.Buffered(1))


def _ffn(x, gain, w_in, w_out, final_gain, final):
    rows = x.shape[0]
    tm = 512
    return pl.pallas_call(
        functools.partial(_ffn_kernel, final=final),
        grid=(rows // tm,),
        in_specs=[pl.BlockSpec((tm, D_MODEL), lambda i: (i, 0)),
                  _resident((1, D_MODEL), lambda i: (0, 0)),
                  _resident((D_MODEL, D_FF), lambda i: (0, 0)),
                  _resident((D_MODEL, D_FF), lambda i: (0, 1)),
                  _resident((D_FF, D_MODEL), lambda i: (0, 0)),
                  _resident((1, D_MODEL), lambda i: (0, 0))],
        out_specs=pl.BlockSpec((tm, D_MODEL), lambda i: (i, 0)),
        out_shape=jax.ShapeDtypeStruct((rows, D_MODEL), F32),
        compiler_params=_params("parallel"),
        name="ffn_final" if final else "ffn",
    )(x, gain, w_in, w_in, w_out, final_gain)


def _proj_kernel(x_ref, g_ref, w_ref, sc_ref, cos_ref, sin_ref, dil_ref, rope_ref, plain_ref):
    n = _rmsnorm(x_ref[...], g_ref[...], NORM_EPS).astype(BF16)
    rot = jnp.dot(n, w_ref[:, :ROTARY_COLS], preferred_element_type=F32) * sc_ref[:, :ROTARY_COLS]
    rest = jnp.dot(n, w_ref[:, ROTARY_COLS:], preferred_element_type=F32) * sc_ref[:, ROTARY_COLS:]
    cos = cos_ref[...]
    sin = sin_ref[...]
    first_half = lax.broadcasted_iota(jnp.int32, cos.shape, 1) % HEAD_DIM < ROT_HALF
    for c in range(ROTARY_COLS // LANES):
        t = rot[:, c * LANES:(c + 1) * LANES]
        partner = jnp.where(first_half,
                            pltpu.roll(t, LANES - ROT_HALF, axis=1),
                            pltpu.roll(t, ROT_HALF, axis=1))
        roped = t * cos + partner * sin
        lo = c * LANES
        if lo < D_VB:
            dil_ref[:, lo:lo + LANES] = roped
        else:
            rope_ref[:, lo - D_VB:lo - D_VB + LANES] = roped.astype(rope_ref.dtype)
    dil_ref[:, D_VB:] = rest[:, :DIL_WIDTH]
    plain_ref[...] = rest[:, DIL_WIDTH:].astype(plain_ref.dtype)


def _proj(x, gain, w, scale, tables):
    rows = x.shape[0]
    tm = 512
    row = lambda width: pl.BlockSpec((tm, width), lambda i: (i, 0))
    const = lambda a: _resident(a.shape, lambda i: (0, 0))
    return pl.pallas_call(
        _proj_kernel,
        grid=(rows // tm,),
        in_specs=[row(D_MODEL), const(gain), const(w), const(scale), row(LANES), row(LANES)],
        out_specs=[row(DIL_COLS), row(ROPE_COLS), row(PLAIN_COLS)],
        out_shape=[jax.ShapeDtypeStruct((rows, DIL_COLS), F32),
                   jax.ShapeDtypeStruct((rows, ROPE_COLS), BF16),
                   jax.ShapeDtypeStruct((rows, PLAIN_COLS), BF16)],
        compiler_params=_params("parallel"),
        name="proj",
    )(x, gain, w, scale, *tables)


def _nt_dot(a, b):
    return lax.dot_general(a, b, (((1,), (1,)), ((), ())), preferred_element_type=F32)


def _tn_dot(a, b):
    return lax.dot_general(a, b, (((0,), (0,)), ((), ())), preferred_element_type=F32)


def _sb_kernel(q_ref, k_ref, v_ref, o_ref, *, tq, npairs):
    lane = lax.broadcasted_iota(jnp.int32, (1, LANES), 1)
    halves = (lane < HEAD_DIM, lane >= HEAD_DIM)
    chains = [(pp, h) for pp in range(npairs) for h in range(2)]
    pair_cols = [slice(pp * LANES, (pp + 1) * LANES) for pp in range(npairs)]
    key = lax.broadcasted_iota(jnp.int32, (tq, tq), 0)
    qry = lax.broadcasted_iota(jnp.int32, (tq, tq), 1)
    later_key = (qry > key).astype(BF16)
    strict = key < qry
    feat = lax.broadcasted_iota(jnp.int32, (LANES, 1), 0)

    def query_block(i, _):
        q_rows = pl.ds(pl.multiple_of(i * tq, tq), tq)
        q_pairs = [q_ref[q_rows, cols] for cols in pair_cols]
        qms = [jnp.where(halves[h], q_pairs[pp], jnp.zeros_like(q_pairs[pp])) for pp, h in chains]
        state = _sb_walk(k_ref, v_ref, qms, i, tq, npairs, chains, pair_cols, later_key, strict)
        for pp, cols in enumerate(pair_cols):
            o_ref[q_rows, cols] = jnp.where(feat < HEAD_DIM, state[2 * pp][0], state[2 * pp + 1][0]).T.astype(o_ref.dtype)
        return 0

    lax.fori_loop(0, q_ref.shape[0] // tq, query_block, 0)


def _sb_walk(k_ref, v_ref, qms, i, tq, npairs, chains, pair_cols, later_key, strict):
    def block(j, state, masked):
        start = pl.multiple_of(j * tq, tq)
        kbs = [k_ref[pl.ds(start, tq), cols] for cols in pair_cols]
        vbs = [v_ref[pl.ds(start, tq), cols] for cols in pair_cols]
        z = [_nt_dot(kbs[pp], qm) for (pp, h), qm in zip(chains, qms)]
        log_beta, rounded, sums = [], [], []
        for x in z:
            u = jnp.maximum(x, 0.0) + jnp.log(1.0 + jnp.exp(-jnp.abs(x)))
            if masked:
                u = jnp.where(strict, u, 0.0)
            rounded.append(u.astype(BF16))
            log_beta.append(x - u)
            sums.append(jnp.sum(u, axis=0, keepdims=True))
        parts = jnp.dot(later_key, jnp.concatenate(rounded, axis=1), preferred_element_type=F32)
        w = []
        for c, (lb, st) in enumerate(zip(log_beta, state)):
            after = parts[:, c * tq:(c + 1) * tq] + st[1]
            wc = jnp.exp(lb - after)
            if masked:
                wc = jnp.where(strict, wc, 0.0)
            w.append(wc.astype(BF16))
        pv = [_tn_dot(vbs[pp], jnp.concatenate(w[2 * pp:2 * pp + 2], axis=1)) for pp in range(npairs)]
        return tuple((st[0] + pv[pp][:, h * tq:(h + 1) * tq], st[1] + y)
                     for (pp, h), st, y in zip(chains, state, sums))

    def smallest_carry(state):
        low = state[0][1]
        for st in state[1:]:
            low = jnp.minimum(low, st[1])
        return jnp.min(low)

    def more(c):
        j, _, low = c
        return (j >= 0) & (low < SB_NEGLIGIBLE)

    def step(c):
        j, st, _ = c
        st = block(j, st, False)
        return j - 1, st, smallest_carry(st)

    zero = (jnp.zeros((LANES, tq), F32), jnp.zeros((1, tq), F32))
    state = block(i, (zero,) * len(chains), True)
    _, state, _ = lax.while_loop(more, step, (i - 1, state, smallest_carry(state)))
    return state


def _stick_breaking(p_plain, batch, seq):
    tq = 256
    npairs = 4
    width = npairs * LANES
    groups = SB_WIDTH // width
    kcol, vcol = P_KA // width, P_VA // width
    spec = lambda col: pl.BlockSpec((seq, width), lambda b, p: (b, col + p))
    return pl.pallas_call(
        functools.partial(_sb_kernel, tq=tq, npairs=npairs),
        grid=(batch, groups),
        in_specs=[spec(0), spec(kcol), spec(vcol)],
        out_specs=spec(0),
        out_shape=jax.ShapeDtypeStruct((batch * seq, SB_WIDTH), BF16),
        compiler_params=_params("parallel", "parallel"),
        name="stick_breaking",
    )(p_plain, p_plain, p_plain)


def _rows_of(start, size, stride):
    return pl.ds(start, size) if stride == 1 else pl.ds(start, size, stride=stride)


def _dil_kernel(*refs):
    qkv_refs, (y_ref, o_scr, lse_scr) = refs[:-3], refs[-3:]
    span = DIL_SPAN
    seq = y_ref.shape[0]
    lane = lax.broadcasted_iota(jnp.int32, (1, LANES), 1)
    heads = [(lane >= h * HEAD_DIM) & (lane < (h + 1) * HEAD_DIM) for h in range(LANES // HEAD_DIM)]

    def band(nkeys):
        qi = lax.broadcasted_iota(jnp.int32, (span, nkeys), 0) + (nkeys - span)
        kj = lax.broadcasted_iota(jnp.int32, (span, nkeys), 1)
        return (qi - kj >= 0) & (qi - kj <= span)

    bands = {span: band(span), 2 * span: band(2 * span)}

    for g, (window, dil) in enumerate(DIL_PATTERNS):
        q_ref, k_ref, v_ref = qkv_refs[3 * g:3 * g + 3]

        def tile_rows(r, c):
            return _rows_of(r * span * dil + c, span, dil)

        def keys_of(ref, r, c):
            own = ref[tile_rows(r, c), :]
            if r == 0:
                return own.astype(BF16)
            return jnp.concatenate([ref[tile_rows(r - 1, c), :], own], axis=0).astype(BF16)

        all_tiles = [(r, c) for r in range(seq // (span * dil)) for c in range(dil)]
        per_pass = 8
        for t0 in range(0, len(all_tiles), per_pass):
            tiles = all_tiles[t0:t0 + per_pass]
            chains = [(t, h) for t in range(len(tiles)) for h in range(len(heads))]
            qs = [q_ref[tile_rows(r, c), :].astype(BF16) for r, c in tiles]
            ks = [keys_of(k_ref, r, c) for r, c in tiles]
            vs = [keys_of(v_ref, r, c) for r, c in tiles]
            s = [_nt_dot(jnp.where(heads[h], qs[t], jnp.zeros_like(qs[t])), ks[t]) for t, h in chains]
            s = [jnp.where(bands[x.shape[1]], x, -jnp.inf) for x in s]
            m = [jnp.max(x, axis=-1, keepdims=True) for x in s]
            p = [jnp.exp(x - mx) for x, mx in zip(s, m)]
            l = [jnp.sum(x, axis=-1, keepdims=True) for x in p]
            o = [jnp.dot(x.astype(BF16), vs[t], preferred_element_type=F32) for x, (t, h) in zip(p, chains)]
            o = [x * (1.0 / lx) for x, lx in zip(o, l)]
            lse = [mx + jnp.log(lx) for mx, lx in zip(m, l)]
            for t, (r, c) in enumerate(tiles):
                out = jnp.zeros((span, LANES), F32)
                lse_out = jnp.zeros((span, LANES), F32)
                for h in range(len(heads)):
                    out = jnp.where(heads[h], o[t * len(heads) + h], out)
                    lse_out = jnp.where(heads[h], lse[t * len(heads) + h], lse_out)
                o_scr[g, tile_rows(r, c), :] = out
                lse_scr[g, tile_rows(r, c), :] = lse_out

    chunk = 2 * span
    for r0 in range(0, seq, chunk):
        rows = slice(r0, r0 + chunk)
        lses = [lse_scr[g, rows, :] for g in range(len(DIL_PATTERNS))]
        top = functools.reduce(jnp.maximum, lses)
        es = [jnp.exp(x - top) for x in lses]
        inv = 1.0 / functools.reduce(lambda a, b: a + b, es)
        y = functools.reduce(lambda a, b: a + b, [(e * inv) * o_scr[g, rows, :] for g, e in enumerate(es)])
        y_ref[rows, :] = y.astype(y_ref.dtype)


def _dilated(p_dil, batch, seq):
    pairs = DIL_GROUP_WIDTH // LANES
    groups = len(DIL_PATTERNS)
    assert all(window // dil == DIL_SPAN and seq % (DIL_SPAN * dil) == 0 for window, dil in DIL_PATTERNS)
    spec = lambda col: pl.BlockSpec((seq, LANES), lambda b, u: (b, col + u))
    in_specs = [spec(c // LANES + g * pairs) for g in range(groups) for c in (D_QB, D_KB, D_VB)]
    return pl.pallas_call(
        _dil_kernel,
        grid=(batch, pairs),
        in_specs=in_specs,
        out_specs=spec(0),
        out_shape=jax.ShapeDtypeStruct((batch * seq, DIL_GROUP_WIDTH), BF16),
        scratch_shapes=[pltpu.VMEM((groups, seq, LANES), F32), pltpu.VMEM((groups, seq, LANES), F32)],
        compiler_params=_params("parallel", "parallel"),
        name="dilated",
    )(*([p_dil] * len(in_specs)))


def _diff_kernel(q1_ref, q2_ref, k1_ref, k2_ref, v_ref, lq1_ref, lk1_ref, lq2_ref, lk2_ref, g_ref,
                 o_ref, *, tq, npairs, lam_init):
    lane = lax.broadcasted_iota(jnp.int32, (1, LANES), 1)
    halves = (lane < HEAD_DIM, lane >= HEAD_DIM)
    heads = [(pp, h) for pp in range(npairs) for h in range(2)]
    chains = [(hd, t) for hd in range(len(heads)) for t in range(2)]
    q_refs = (q1_ref, q2_ref)
    k_refs = (k1_ref, k2_ref)
    pair_cols = [slice(pp * LANES, (pp + 1) * LANES) for pp in range(npairs)]
    head_cols = [slice(hd * DIFF_V_DIM, (hd + 1) * DIFF_V_DIM) for hd in range(len(heads))]
    key = lax.broadcasted_iota(jnp.int32, (tq, tq), 0)
    qry = lax.broadcasted_iota(jnp.int32, (tq, tq), 1)
    causal = key <= qry

    def query_block(i, _):
        q_rows = pl.ds(pl.multiple_of(i * tq, tq), tq)
        q_pairs = [[r[q_rows, cols] for cols in pair_cols] for r in q_refs]
        qs = [jnp.where(halves[heads[hd][1]], q_pairs[t][heads[hd][0]], jnp.zeros_like(q_pairs[t][heads[hd][0]]))
              for hd, t in chains]

        def scores(j, nk):
            start = pl.multiple_of(j * tq, tq)
            kbs = [[r[pl.ds(start, nk), cols] for cols in pair_cols] for r in k_refs]
            return [_nt_dot(kbs[t][heads[hd][0]], q) for (hd, t), q in zip(chains, qs)]

        def absorb(s, j, nk, state, masked):
            start = pl.multiple_of(j * tq, tq)
            vbs = [v_ref[pl.ds(start, nk), cols] for cols in head_cols]
            if masked:
                s = [jnp.where(causal, x, -jnp.inf) for x in s]
            m_new = [jnp.maximum(st[0], jnp.max(x, axis=0, keepdims=True)) for st, x in zip(state, s)]
            pb, psum = [], []
            for x, mx in zip(s, m_new):
                p = jnp.exp(x - mx)
                pb.append(p.astype(BF16))
                psum.append(jnp.sum(p, axis=0, keepdims=True))
            pv = [_tn_dot(vbs[hd], jnp.concatenate(pb[2 * hd:2 * hd + 2], axis=1)) for hd in range(len(heads))]
            pv = [pv[hd][:, t * tq:(t + 1) * tq] for hd, t in chains]
            alpha = [jnp.exp(st[0] - mx) for st, mx in zip(state, m_new)]
            l_new = [a * st[1] + y for a, st, y in zip(alpha, state, psum)]
            acc = [a * st[2] + x for a, st, x in zip(alpha, state, pv)]
            return tuple(zip(m_new, l_new, acc))

        init = (jnp.full((1, tq), -jnp.inf, F32), jnp.zeros((1, tq), F32), jnp.zeros((DIFF_V_DIM, tq), F32))
        state = absorb(scores(i, tq), i, tq, (init,) * len(chains), True)
        state = lax.cond(i % 2 == 1, lambda st: absorb(scores(i - 1, tq), i - 1, tq, st, False),
                         lambda st: st, state)
        state = lax.fori_loop(0, i // 2,
                              lambda jj, st: absorb(scores(2 * jj, 2 * tq), 2 * jj, 2 * tq, st, False), state)

        lam = (jnp.exp(jnp.sum(lq1_ref[...] * lk1_ref[...], axis=-1, keepdims=True))
               - jnp.exp(jnp.sum(lq2_ref[...] * lk2_ref[...], axis=-1, keepdims=True)) + lam_init)
        for hd, cols in enumerate(head_cols):
            (_, l1, a1), (_, l2, a2) = state[2 * hd], state[2 * hd + 1]
            o = (a1 * (1.0 / l1) - lam * (a2 * (1.0 / l2))).T
            o_ref[q_rows, cols] = (_rmsnorm(o, g_ref[...], SUBLN_EPS) * (1.0 - lam_init)).astype(o_ref.dtype)
        return 0

    lax.fori_loop(0, o_ref.shape[0] // tq, query_block, 0)


def _diff_attention(p_rope, p_plain, lam_vecs, subln, batch, seq, lam_init):
    tq = 256
    npairs = 2
    qk_w = npairs * LANES
    v_w = 2 * npairs * DIFF_V_DIM
    groups = DIFF_QK_WIDTH // qk_w
    q1c, q2c, k1c, k2c = (c // qk_w for c in (R_Q1, R_Q2, R_K1, R_K2))
    vcol = P_VC // v_w
    qkspec = lambda c0: pl.BlockSpec((seq, qk_w), lambda b, p: (b, c0 + p))
    vec = lambda w: pl.BlockSpec((1, w), lambda b, p: (0, 0))
    return pl.pallas_call(
        functools.partial(_diff_kernel, tq=tq, npairs=npairs, lam_init=lam_init),
        grid=(batch, groups),
        in_specs=[qkspec(q1c), qkspec(q2c), qkspec(k1c), qkspec(k2c),
                  pl.BlockSpec((seq, v_w), lambda b, p: (b, vcol + p)),
                  vec(HEAD_DIM), vec(HEAD_DIM), vec(HEAD_DIM), vec(HEAD_DIM), vec(DIFF_V_DIM)],
        out_specs=pl.BlockSpec((seq, v_w), lambda b, p: (b, p)),
        out_shape=jax.ShapeDtypeStruct((batch * seq, DIFF_V_WIDTH), BF16),
        compiler_params=_params("parallel", "parallel"),
        name="diff_attention",
    )(p_rope, p_rope, p_rope, p_rope, p_plain, *lam_vecs, subln)


def _mix_kernel(x_ref, g_ref, wg_ref, bg_ref, ya_ref, yb_ref, yc_ref,
                wa_ref, wb_ref, wc_ref, wo_ref, out_ref):
    x = x_ref[...]
    hn = _rmsnorm(x, g_ref[...], NORM_EPS).astype(BF16)
    gates = jax.nn.sigmoid(jnp.dot(hn, wg_ref[...], preferred_element_type=F32) + bg_ref[...])
    up_a = jnp.dot(ya_ref[...], wa_ref[...], preferred_element_type=F32)
    up_b = jnp.dot(yb_ref[...], wb_ref[...], preferred_element_type=F32)
    up_c = jnp.dot(yc_ref[...], wc_ref[...], preferred_element_type=F32)
    d = D_MODEL
    merged = gates[:, :d] * up_a + gates[:, d:2 * d] * up_b + gates[:, 2 * d:] * up_c
    out_ref[...] = x + jnp.dot(merged.astype(BF16), wo_ref[...], preferred_element_type=F32)


def _mix_out(x, gain, w_gate, b_gate, y_a, y_b, y_c, w_up_a, w_up_b, w_up_c, w_out):
    rows = x.shape[0]
    tm = 512
    row = lambda w: pl.BlockSpec((tm, w), lambda i: (i, 0))
    full = lambda a: _resident(a.shape, lambda i: (0, 0))
    return pl.pallas_call(
        _mix_kernel,
        grid=(rows // tm,),
        in_specs=[row(D_MODEL), full(gain), full(w_gate), full(b_gate),
                  row(SB_WIDTH), row(DIL_GROUP_WIDTH), row(DIFF_V_WIDTH),
                  full(w_up_a), full(w_up_b), full(w_up_c), full(w_out)],
        out_specs=row(D_MODEL),
        out_shape=jax.ShapeDtypeStruct((rows, D_MODEL), F32),
        compiler_params=_params("parallel"),
        name="mix_out",
    )(x, gain, w_gate, b_gate, y_a, y_b, y_c, w_up_a, w_up_b, w_up_c, w_out)


def _column_scale(width, scaled_ranges):
    col = jnp.arange(width)
    s = jnp.ones((width,), F32)
    for lo, hi in scaled_ranges:
        s = jnp.where((col >= lo) & (col < hi), QK_SCALE, s)
    return s.reshape(1, width)


def kernel(x, positions, ffn1_norm, ffn1_w_in, ffn1_w_out, mix_norm, w_in, b_gate, lam_q1, lam_k1, lam_q2, lam_k2, diff_subln, w_up_a, w_up_b, w_up_c, w_out, ffn2_norm, ffn2_w_in, ffn2_w_out, final_norm):
    batch, seq, d = x.shape
    rows = batch * seq
    x = x.reshape(rows, d)
    tables = _rope_tables(positions)
    vec = lambda t: t.reshape(1, -1)

    o_qa, o_ka, o_va = 0, SB_WIDTH, 2 * SB_WIDTH
    o_qb = 3 * SB_WIDTH
    o_kb, o_vb = o_qb + DIL_WIDTH, o_qb + 2 * DIL_WIDTH
    o_q1 = o_qb + 3 * DIL_WIDTH
    o_vc = o_q1 + 4 * DIFF_QK_WIDTH
    o_gate = o_vc + DIFF_V_WIDTH
    o_qa_new = ROTARY_COLS + DIL_WIDTH
    scale = _column_scale(QKV_COLS, [(0, DIL_WIDTH),
                                     (2 * DIL_WIDTH, 2 * DIL_WIDTH + 2 * DIFF_QK_WIDTH),
                                     (o_qa_new, o_qa_new + SB_WIDTH)])

    for l in range(DEPTH):
        wl = w_in[l]
        w_qkv = jnp.concatenate([wl[:, o_qb:o_vb], wl[:, o_q1:o_vc], wl[:, o_vb:o_q1],
                                 wl[:, o_qa:o_qb], wl[:, o_vc:o_gate]], axis=1).astype(BF16)
        w_gate = wl[:, o_gate:].astype(BF16)

        x = _ffn(x, vec(ffn1_norm[l]), ffn1_w_in[l].astype(BF16), ffn1_w_out[l].astype(BF16),
                 vec(final_norm), False)

        gain = vec(mix_norm[l])
        p_dil, p_rope, p_plain = _proj(x, gain, w_qkv, scale, tables)

        y_a = _stick_breaking(p_plain, batch, seq)

        y_b = _dilated(p_dil, batch, seq)

        lam_init = 0.8 - 0.6 * math.exp(-0.3 * l)
        y_c = _diff_attention(p_rope, p_plain,
                              [vec(t[l]) for t in (lam_q1, lam_k1, lam_q2, lam_k2)],
                              vec(diff_subln[l]), batch, seq, lam_init)

        x = _mix_out(x, gain, w_gate, vec(b_gate[l]), y_a, y_b, y_c,
                     w_up_a[l].astype(BF16), w_up_b[l].astype(BF16), w_up_c[l].astype(BF16),
                     w_out[l].astype(BF16))

        x = _ffn(x, vec(ffn2_norm[l]), ffn2_w_in[l].astype(BF16), ffn2_w_out[l].astype(BF16),
                 vec(final_norm), l == DEPTH - 1)
    return x.reshape(batch, seq, d)
```
